```python
import math
import jax, jax.numpy as jnp
from jax import lax
import numpy as np

D_MODEL = 1024
BATCH = 32
SEQ = 2048
DEPTH = 1
DEC_BATCH = 16
DEC_SEQ = 4096
PAST_LEN = 128

N_HEADS_ATTN = 8
QK_NOPE_DIM = 64
QK_ROPE_DIM = 32
V_HEAD_DIM = 64
Q_LORA_RANK = 256
KV_LORA_RANK = 128
ROPE_THETA = 10000.0
Q_BLOCK = 128
D_ATTN = N_HEADS_ATTN * V_HEAD_DIM
N_HEADS_SSD = 8
SSD_HEAD_DIM = 64
D_SSD = N_HEADS_SSD * SSD_HEAD_DIM
SSD_GROUPS = 2
D_STATE = 64
D_CONV = 5
CHUNK = 128
CONV_DIM = D_SSD + 2 * SSD_GROUPS * D_STATE
D_MIX = D_ATTN + D_SSD
D_IN_PROJ = Q_LORA_RANK + KV_LORA_RANK + QK_ROPE_DIM + D_SSD + CONV_DIM + 2 * N_HEADS_SSD
N_EXPERTS = 16
CAPACITY_FACTOR = 2
D_EXPERT = 2048
N_MOD = 6
EPS = 1e-6

kernel_name = 'hymba_mla_ssd_ec_moe_encoder'


def _rms(x):
    xf = x.astype(jnp.float32)
    return xf * lax.rsqrt(jnp.mean(xf * xf, axis=-1, keepdims=True) + EPS)


def _rmsnorm(x, w):
    return (_rms(x) * w.astype(jnp.float32)).astype(x.dtype)


def _rope_tables(seq, dtype):
    pos = jnp.arange(seq, dtype=jnp.float32)
    inv = 1.0 / (ROPE_THETA ** (jnp.arange(0, QK_ROPE_DIM, 2, dtype=jnp.float32) / QK_ROPE_DIM))
    ang = pos[:, None] * inv[None, :]
    return jnp.cos(ang).astype(dtype), jnp.sin(ang).astype(dtype)


def _rope(x, cos, sin):
    x1, x2 = jnp.split(x, 2, axis=-1)
    return jnp.concatenate([x1 * cos - x2 * sin, x2 * cos + x1 * sin], axis=-1)


def _mla(cq, ckv, kr, q_norm_w, w_uq, kv_norm_w, w_ukv, attn_norm_w):
    b, s, _ = cq.shape
    cos, sin = _rope_tables(s, cq.dtype)
    q = (_rmsnorm(cq, q_norm_w) @ w_uq).reshape(b, s, N_HEADS_ATTN, QK_NOPE_DIM + QK_ROPE_DIM)
    q_nope, q_rope = jnp.split(q, [QK_NOPE_DIM], axis=-1)
    q_rope = _rope(q_rope, cos[:, None, :], sin[:, None, :])
    kv = (_rmsnorm(ckv, kv_norm_w) @ w_ukv).reshape(b, s, N_HEADS_ATTN, QK_NOPE_DIM + V_HEAD_DIM)
    k_nope, v = jnp.split(kv, [QK_NOPE_DIM], axis=-1)
    k_rope = _rope(kr, cos, sin)
    scale = (QK_NOPE_DIM + QK_ROPE_DIM) ** -0.5
    nb = s // Q_BLOCK

    def to_blocks(t):
        return jnp.moveaxis(t.reshape(b, nb, Q_BLOCK, N_HEADS_ATTN, t.shape[-1]), 1, 0)

    def attend(blk):
        qn, qr = blk
        sc = jnp.einsum('bqhd,bkhd->bhqk', qn, k_nope) + jnp.einsum('bqhr,bkr->bhqk', qr, k_rope)
        p = jax.nn.softmax(sc.astype(jnp.float32) * scale, axis=-1).astype(v.dtype)
        return jnp.einsum('bhqk,bkhd->bqhd', p, v)

    o = lax.map(attend, (to_blocks(q_nope), to_blocks(q_rope)))
    o = jnp.moveaxis(o, 0, 1).reshape(b, s, D_ATTN)
    return _rmsnorm(o, attn_norm_w)


def _dwconv(u, w, bias):
    out = lax.conv_general_dilated(u, w[:, None, :], window_strides=(1,),
                                   padding=[(D_CONV // 2, D_CONV // 2)],
                                   dimension_numbers=('NWC', 'WIO', 'NWC'),
                                   feature_group_count=u.shape[-1])
    return out + bias


def _ssd_scan(x, dt, a, bm, cm):
    b, l, nh, p = x.shape
    g, n = bm.shape[2], bm.shape[3]
    r = nh // g
    nc = l // CHUNK
    x = x.astype(jnp.float32).reshape(b, nc, CHUNK, g, r, p)
    dt = dt.reshape(b, nc, CHUNK, g, r)
    bm = bm.astype(jnp.float32).reshape(b, nc, CHUNK, g, n)
    cm = cm.astype(jnp.float32).reshape(b, nc, CHUNK, g, n)
    cs = jnp.cumsum(dt * a.reshape(g, r), axis=2)
    lower = jnp.tril(jnp.ones((CHUNK, CHUNK), bool))[None, None, :, :, None, None]
    seg = cs[:, :, :, None] - cs[:, :, None, :]
    decay_in = jnp.exp(jnp.where(lower, seg, -jnp.inf))
    cb = jnp.einsum('bcqgn,bcsgn->bcqsg', cm, bm)
    m = cb[..., None] * decay_in * dt[:, :, None]
    y_diag = jnp.einsum('bcqsgr,bcsgrp->bcqgrp', m, x)
    decay_out = jnp.exp(cs[:, :, -1:] - cs)
    states = jnp.einsum('bcsgn,bcsgr,bcsgrp->bcgrpn', bm, decay_out * dt, x)
    chunk_decay = jnp.exp(cs[:, :, -1])

    def step(carry, inp):
        st, dec = inp
        return carry * dec[..., None, None] + st, carry

    init = jnp.zeros((b, g, r, p, n), jnp.float32)
    _, s_in = lax.scan(step, init, (jnp.moveaxis(states, 1, 0), jnp.moveaxis(chunk_decay, 1, 0)))
    s_in = jnp.moveaxis(s_in, 0, 1)
    y_off = jnp.einsum('bcqgn,bcgrpn->bcqgrp', cm, s_in) * jnp.exp(cs)[..., None]
    return (y_diag + y_off).reshape(b, l, nh, p)


def _ssd_mixer(z, xbc, dt_raw, conv_w, conv_b, a_log, dt_bias, d_skip, ssd_norm_w):
    b, s, _ = z.shape
    f32 = jnp.float32
    xbc = jax.nn.silu(_dwconv(xbc, conv_w, conv_b))
    xs, bm, cm = jnp.split(xbc, [D_SSD, D_SSD + SSD_GROUPS * D_STATE], axis=-1)
    xh = xs.reshape(b, s, N_HEADS_SSD, SSD_HEAD_DIM)
    bm = bm.reshape(b, s, SSD_GROUPS, D_STATE)
    cm = cm.reshape(b, s, SSD_GROUPS, D_STATE)
    dt = jax.nn.softplus(dt_raw.astype(f32).reshape(b, s, 2, N_HEADS_SSD) + dt_bias.astype(f32))
    a = -jnp.exp(a_log.astype(f32))
    flip = lambda t: jnp.flip(t, axis=1)
    y_fwd = _ssd_scan(xh, dt[:, :, 0], a[0], bm, cm)
    y_bwd = flip(_ssd_scan(flip(xh), flip(dt[:, :, 1]), a[1], flip(bm), flip(cm)))
    y = y_fwd + y_bwd + d_skip.astype(f32)[:, None] * xh.astype(f32)
    y = y.reshape(b, s, D_SSD) * jax.nn.silu(z.astype(f32))
    y = _rms(y.reshape(b, s, SSD_GROUPS, D_SSD // SSD_GROUPS)).reshape(b, s, D_SSD)
    return (y * ssd_norm_w.astype(f32)).astype(z.dtype)


def _ec_moe(h, w_router, w_gate, w_up, w_down):
    b, s, d = h.shape
    t = b * s
    tok = h.reshape(t, d)
    aff = jax.nn.softmax((tok @ w_router).astype(jnp.float32), axis=-1)
    cap = max(1, CAPACITY_FACTOR * t // N_EXPERTS)
    gate, idx = lax.top_k(aff.T, cap)
    xe = tok[idx]
    hid = jax.nn.silu(jnp.einsum('ecd,edf->ecf', xe, w_gate)) * jnp.einsum('ecd,edf->ecf', xe, w_up)
    ye = jnp.einsum('ecf,efd->ecd', hid, w_down) * gate[..., None].astype(h.dtype)
    out = jnp.zeros_like(tok).at[idx.reshape(-1)].add(ye.reshape(-1, d))
    return out.reshape(b, s, d)


def _layer(x, c, w_ada, b_ada, norm1_w, w_in, q_norm_w, w_uq, kv_norm_w, w_ukv, attn_norm_w,
           conv_w, conv_b, a_log, dt_bias, d_skip, ssd_norm_w, w_out, norm2_w,
           w_router, w_gate, w_up, w_down):
    mod = jax.nn.silu(c) @ w_ada + b_ada
    sh1, sc1, g1, sh2, sc2, g2 = jnp.split(mod[:, None, :], N_MOD, axis=-1)
    h = _rmsnorm(x, norm1_w) * (1.0 + sc1) + sh1
    proj = h @ w_in
    offs = np.cumsum([Q_LORA_RANK, KV_LORA_RANK, QK_ROPE_DIM, D_SSD, CONV_DIM]).tolist()
    cq, ckv, kr, z, xbc, dt_raw = jnp.split(proj, offs, axis=-1)
    y_attn = _mla(cq, ckv, kr, q_norm_w, w_uq, kv_norm_w, w_ukv, attn_norm_w)
    y_ssd = _ssd_mixer(z, xbc, dt_raw, conv_w, conv_b, a_log, dt_bias, d_skip, ssd_norm_w)
    x = x + g1 * (jnp.concatenate([y_attn, y_ssd], axis=-1) @ w_out)
    h = _rmsnorm(x, norm2_w) * (1.0 + sc2) + sh2
    return x + g2 * _ec_moe(h, w_router, w_gate, w_up, w_down)


def setup_inputs(seed: int = 0) -> dict:
    key = jax.random.key(seed)
    ks = jax.random.split(key, 32)
    f32 = jnp.float32
    nrm = lambda k, shape, sc: jax.random.normal(k, shape, f32) * sc
    gain = lambda k, shape: 1.0 + 0.05 * jax.random.normal(k, shape, f32)
    L = DEPTH
    dt0 = jnp.exp(jax.random.uniform(ks[14], (L, 2, N_HEADS_SSD), f32, math.log(1e-3), math.log(1e-1)))
    return {
        'x_prompt': nrm(ks[0], (BATCH, SEQ, D_MODEL), 1.0),
        'x_sample': nrm(ks[1], (DEC_BATCH, DEC_SEQ, D_MODEL), 1.0),
        'c_prompt': nrm(ks[2], (BATCH, D_MODEL), 1.0),
        'c_sample': nrm(ks[3], (DEC_BATCH, D_MODEL), 1.0),
        'w_ada': nrm(ks[4], (L, D_MODEL, N_MOD * D_MODEL), 0.5 * D_MODEL ** -0.5),
        'b_ada': nrm(ks[5], (L, N_MOD * D_MODEL), 0.01),
        'norm1_w': gain(ks[6], (L, D_MODEL)),
        'w_in': nrm(ks[7], (L, D_MODEL, D_IN_PROJ), D_MODEL ** -0.5),
        'q_norm_w': gain(ks[8], (L, Q_LORA_RANK)),
        'w_uq': nrm(ks[9], (L, Q_LORA_RANK, N_HEADS_ATTN * (QK_NOPE_DIM + QK_ROPE_DIM)), Q_LORA_RANK ** -0.5),
        'kv_norm_w': gain(ks[10], (L, KV_LORA_RANK)),
        'w_ukv': nrm(ks[11], (L, KV_LORA_RANK, N_HEADS_ATTN * (QK_NOPE_DIM + V_HEAD_DIM)), KV_LORA_RANK ** -0.5),
        'attn_norm_w': gain(ks[12], (L, D_ATTN)),
        'conv_w': nrm(ks[13], (L, D_CONV, CONV_DIM), D_CONV ** -0.5),
        'conv_b': nrm(ks[15], (L, CONV_DIM), 0.01),
        'a_log': jnp.log(jax.random.uniform(ks[16], (L, 2, N_HEADS_SSD), f32, 1.0, 16.0)),
        'dt_bias': dt0 + jnp.log(-jnp.expm1(-dt0)),
        'd_skip': gain(ks[17], (L, N_HEADS_SSD)),
        'ssd_norm_w': gain(ks[18], (L, D_SSD)),
        'w_out': nrm(ks[19], (L, D_MIX, D_MODEL), D_MIX ** -0.5),
        'norm2_w': gain(ks[20], (L, D_MODEL)),
        'w_router': nrm(ks[21], (L, D_MODEL, N_EXPERTS), D_MODEL ** -0.5),
        'w_gate': nrm(ks[22], (L, N_EXPERTS, D_MODEL, D_EXPERT), D_MODEL ** -0.5),
        'w_up': nrm(ks[23], (L, N_EXPERTS, D_MODEL, D_EXPERT), D_MODEL ** -0.5),
        'w_down': nrm(ks[24], (L, N_EXPERTS, D_EXPERT, D_MODEL), D_EXPERT ** -0.5),
        'final_norm_w': gain(ks[25], (D_MODEL,)),
    }


def reference(x_prompt, x_sample, c_prompt, c_sample, w_ada, b_ada, norm1_w, w_in,
              q_norm_w, w_uq, kv_norm_w, w_ukv, attn_norm_w, conv_w, conv_b, a_log,
              dt_bias, d_skip, ssd_norm_w, w_out, norm2_w, w_router, w_gate, w_up,
              w_down, final_norm_w):
    layer_params = (w_ada, b_ada, norm1_w, w_in, q_norm_w, w_uq, kv_norm_w, w_ukv, attn_norm_w,
                    conv_w, conv_b, a_log, dt_bias, d_skip, ssd_norm_w, w_out, norm2_w,
                    w_router, w_gate, w_up, w_down)

    def trunk(x, c):
        for i in range(DEPTH):
            x = _layer(x, c, *[prm[i] for prm in layer_params])
        return _rmsnorm(x, final_norm_w)

    y_prompt = trunk(x_prompt, c_prompt)
    y_sample = trunk(x_sample, c_sample)
    return (y_prompt, y_sample)
```

```python
import functools
import math

import jax
import jax.numpy as jnp
from jax import lax
from jax.experimental import pallas as pl
from jax.experimental.pallas import tpu as pltpu

F32 = jnp.float32
BF16 = jnp.bfloat16
I32 = jnp.int32
HIGHEST = lax.Precision.HIGHEST

D_MODEL = 1024
N_HEADS = 8
NOPE = 64
ROPE = 32
V_DIM = 64
Q_RANK = 256
KV_RANK = 128
ROPE_THETA = 10000.0
D_ATTN = N_HEADS * V_DIM
SSD_HEADS = 8
SSD_P = 64
D_SSD = SSD_HEADS * SSD_P
SSD_G = 2
SSD_N = 64
D_CONV = 5
CHUNK = 128
CONV_DIM = D_SSD + 2 * SSD_G * SSD_N
N_EXPERTS = 16
CAP_FACTOR = 2
D_EXPERT = 2048
EPS = 1e-6

LANE = 128
HEAD_PAD = 128
BF16_ROWS = 16
VMEM_LIMIT = 56 * 1024 * 1024

_C_CQ = 0
_C_CKV = _C_CQ + Q_RANK
_C_KR = _C_CKV + KV_RANK
_C_KRR = _C_KR + LANE
_C_Z = _C_KRR + LANE
_C_XBC = _C_Z + D_SSD
_C_DT = _C_XBC + CONV_DIM
_N_IN = _C_DT + LANE


def _cparams(sem, vmem=VMEM_LIMIT):
    return pltpu.CompilerParams(dimension_semantics=sem, vmem_limit_bytes=vmem)


def _silu(x):
    return x / (1.0 + jnp.exp(-x))


def _rms(x):
    return x * lax.rsqrt(jnp.mean(x * x, axis=-1, keepdims=True) + EPS)


def _mod_kernel(c_ref, w_ref, b_ref, o_ref):
    a = _silu(c_ref[...])
    o_ref[...] = jnp.dot(a, w_ref[...], precision=HIGHEST, preferred_element_type=F32) + b_ref[...]


def _mod(c, w_ada, b_ada):
    nb, d = c.shape
    n = w_ada.shape[1]
    tn = 1536
    return pl.pallas_call(
        _mod_kernel,
        grid=(n // tn,),
        in_specs=[
            pl.BlockSpec((nb, d), lambda j: (0, 0)),
            pl.BlockSpec((d, tn), lambda j: (0, j)),
            pl.BlockSpec((1, tn), lambda j: (0, j)),
        ],
        out_specs=pl.BlockSpec((nb, tn), lambda j: (0, j)),
        out_shape=jax.ShapeDtypeStruct((nb, n), F32),
        compiler_params=_cparams(("arbitrary",)),
        name="mod",
    )(c, w_ada, b_ada.reshape(1, n))


def _inproj_kernel(x_ref, sc_ref, sh_ref, n1_ref, win_ref, qn_ref, wq_ref, kvn_ref, wkv_ref, tab_ref,
                   q_ref, k_ref, v_ref, z_ref, xbc_ref, dt_ref):
    h = _rms(x_ref[0]) * n1_ref[...]
    h = h * (1.0 + sc_ref[0]) + sh_ref[0]
    proj = jnp.dot(h.astype(BF16), win_ref[...], preferred_element_type=F32)
    z_ref[0] = proj[:, _C_Z:_C_XBC]
    xbc_ref[0] = proj[:, _C_XBC:_C_DT]
    dt_ref[0] = proj[:, _C_DT:_N_IN]

    tab = tab_ref[...]
    cosq, sinq = tab[:, 0:LANE], tab[:, LANE:2 * LANE]
    cosk, sink = tab[:, 2 * LANE:3 * LANE], tab[:, 3 * LANE:4 * LANE]

    cqn = (_rms(proj[:, _C_CQ:_C_CKV]) * qn_ref[...]).astype(BF16)
    q2 = jnp.dot(cqn, wq_ref[...], preferred_element_type=F32)
    ckvn = (_rms(proj[:, _C_CKV:_C_KR]) * kvn_ref[...]).astype(BF16)
    kv = jnp.dot(ckvn, wkv_ref[...], preferred_element_type=F32)
    krope = proj[:, _C_KR:_C_KRR] * cosk + proj[:, _C_KRR:_C_Z] * sink
    nq = N_HEADS * HEAD_PAD
    for hd in range(N_HEADS):
        sl = slice(hd * HEAD_PAD, (hd + 1) * HEAD_PAD)
        rot = slice(nq + hd * HEAD_PAD, nq + (hd + 1) * HEAD_PAD)
        q_ref[0, :, sl] = (q2[:, sl] * cosq + q2[:, rot] * sinq).astype(BF16)
        k_ref[0, :, sl] = (kv[:, sl] + krope).astype(BF16)
    v_ref[0] = kv[:, nq:nq + D_ATTN].astype(BF16)


def _inproj(x, sc1, sh1, norm1_w, win_ext, q_norm_w, wq_ext, kv_norm_w, wkv_ext, tab, tm):
    b, s, d = x.shape
    nq = N_HEADS * HEAD_PAD
    vec = lambda n: pl.BlockSpec((1, n), lambda i, j: (0, 0))
    full = lambda a: pl.BlockSpec(a.shape, lambda i, j: (0, 0))
    per_b = pl.BlockSpec((1, 1, d), lambda i, j: (i, 0, 0))
    tok = lambda n: pl.BlockSpec((1, tm, n), lambda i, j: (i, j, 0))
    return pl.pallas_call(
        _inproj_kernel,
        grid=(b, s // tm),
        in_specs=[tok(d), per_b, per_b, vec(d), full(win_ext), vec(Q_RANK), full(wq_ext), vec(KV_RANK),
                  full(wkv_ext), pl.BlockSpec((tm, 4 * LANE), lambda i, j: (j, 0))],
        out_specs=[tok(nq), tok(nq), tok(D_ATTN), tok(D_SSD), tok(CONV_DIM), tok(LANE)],
        out_shape=[
            jax.ShapeDtypeStruct((b, s, nq), BF16),
            jax.ShapeDtypeStruct((b, s, nq), BF16),
            jax.ShapeDtypeStruct((b, s, D_ATTN), BF16),
            jax.ShapeDtypeStruct((b, s, D_SSD), F32),
            jax.ShapeDtypeStruct((b, s, CONV_DIM), F32),
            jax.ShapeDtypeStruct((b, s, LANE), F32),
        ],
        compiler_params=_cparams(("parallel", "arbitrary")),
        name="inproj",
    )(x, sc1, sh1, norm1_w.reshape(1, d), win_ext, q_norm_w.reshape(1, -1), wq_ext,
      kv_norm_w.reshape(1, -1), wkv_ext, tab)


def _attn_kernel(q_ref, k_ref, v_ref, o_ref):
    v = v_ref[0]
    outs = []
    for i in range(2):
        sl = slice(i * HEAD_PAD, (i + 1) * HEAD_PAD)
        s = lax.dot_general(q_ref[0, :, sl], k_ref[0, :, sl], (((1,), (1,)), ((), ())),
                            preferred_element_type=F32)
        m = jnp.max(s, axis=-1, keepdims=True)
        p = jnp.exp2(s - m)
        l = jnp.sum(p, axis=-1, keepdims=True)
        outs.append(jnp.dot(p.astype(BF16), v, preferred_element_type=F32) / l)
    lane = lax.broadcasted_iota(I32, outs[0].shape, 1)
    o_ref[0] = jnp.where(lane < V_DIM, outs[0], outs[1])


def _attention(q, k, v, tq):
    b, s, _ = q.shape
    pairs = N_HEADS // 2
    return pl.pallas_call(
        _attn_kernel,
        grid=(b, pairs, s // tq),
        in_specs=[
            pl.BlockSpec((1, tq, 2 * HEAD_PAD), lambda i, p, j: (i, j, p)),
            pl.BlockSpec((1, s, 2 * HEAD_PAD), lambda i, p, j: (i, 0, p)),
            pl.BlockSpec((1, s, 2 * V_DIM), lambda i, p, j: (i, 0, p)),
        ],
        out_specs=pl.BlockSpec((1, tq, 2 * V_DIM), lambda i, p, j: (i, j, p)),
        out_shape=jax.ShapeDtypeStruct((b, s, D_ATTN), F32),
        compiler_params=_cparams(("parallel", "arbitrary", "arbitrary")),
        name="attn",
    )(q, k, v)


def _ssd_direction(xbc_ref, halo_ref, dt_ref, cw_ref, cb_ref, alog_ref, dtb_ref, dsk_ref, ext_ref, st_ref, y_ref,
                   reverse):
    q = CHUNK
    ext_ref[6:8, :] = halo_ref[0, 0, 0:2, :]
    ext_ref[8:8 + q, :] = xbc_ref[0]
    ext_ref[8 + q:10 + q, :] = halo_ref[0, 0, 2:4, :]
    conv = cb_ref[...] + cw_ref[0:1, :] * ext_ref[6:6 + q, :]
    for kk in range(1, D_CONV):
        conv = conv + cw_ref[kk:kk + 1, :] * ext_ref[6 + kk:6 + kk + q, :]
    act = _silu(conv)
    bm = act[:, D_SSD:D_SSD + LANE]
    cm = act[:, D_SSD + LANE:D_SSD + 2 * LANE]

    off = SSD_HEADS if reverse else 0
    raw = dt_ref[0] + dtb_ref[...]
    dt = jnp.maximum(raw, 0.0) + jnp.log1p(jnp.exp(-jnp.abs(raw)))
    dta = dt * (-jnp.exp(alog_ref[...]))
    ri = lax.broadcasted_iota(I32, (q, q), 0)
    ci = lax.broadcasted_iota(I32, (q, q), 1)
    mask = (ci >= ri) if reverse else (ci <= ri)
    cs = jnp.dot(mask.astype(F32), dta, precision=HIGHEST, preferred_element_type=F32)
    cs_t = cs.T
    dt_t = dt.T
    end = 0 if reverse else q - 1
    cs_end = cs[end:end + 1, :]
    wdec = jnp.exp(cs_end - cs) * dt
    ecs = jnp.exp(cs)
    cdec = jnp.exp(cs_end)

    lane = lax.broadcasted_iota(I32, (q, LANE), 1)
    sub = lax.broadcasted_iota(I32, (LANE, LANE), 0)
    cm16 = cm.astype(BF16)
    cbs = []
    for g in range(SSD_G):
        in_g = (lane >= g * SSD_N) & (lane < (g + 1) * SSD_N)
        cg = jnp.where(in_g, cm, 0.0).astype(BF16)
        cbs.append(lax.dot_general(cg, bm.astype(BF16), (((1,), (1,)), ((), ())), preferred_element_type=F32))

    for j in range(SSD_HEADS // 2):
        g = (2 * j) // (SSD_HEADS // SSD_G)
        xp = act[:, j * LANE:(j + 1) * LANE]
        xp16 = xp.astype(BF16)
        yd = []
        for hh in (2 * j, 2 * j + 1):
            ln = off + hh
            seg = cs[:, ln:ln + 1] - cs_t[ln:ln + 1, :]
            dec = jnp.exp(jnp.where(mask, seg, -jnp.inf))
            m = (cbs[g] * dec * dt_t[ln:ln + 1, :]).astype(BF16)
            yd.append(jnp.dot(m, xp16, preferred_element_type=F32))
        la, lb = off + 2 * j, off + 2 * j + 1
        first = lane < SSD_P
        y = jnp.where(first, yd[0], yd[1])
        st = st_ref[j]
        y_off = lax.dot_general(cm16, st.astype(BF16), (((1,), (1,)), ((), ())), preferred_element_type=F32)
        y = y + y_off * jnp.where(first, ecs[:, la:la + 1], ecs[:, lb:lb + 1])
        xw = (xp * jnp.where(first, wdec[:, la:la + 1], wdec[:, lb:lb + 1])).astype(BF16)
        in_g = (lane >= g * SSD_N) & (lane < (g + 1) * SSD_N)
        bg = jnp.where(in_g, bm, 0.0).astype(BF16)
        new = lax.dot_general(xw, bg, (((0,), (0,)), ((), ())), preferred_element_type=F32)
        st_ref[j] = st * jnp.where(sub < SSD_P, cdec[:, la:la + 1], cdec[:, lb:lb + 1]) + new
        if not reverse:
            y = y + dsk_ref[:, j * LANE:(j + 1) * LANE] * xp
        y_ref[0, :, j * LANE:(j + 1) * LANE] = y


def _ssd_kernel(xf_ref, hf_ref, dtf_ref, xb_ref, hb_ref, dtb_in_ref, cw_ref, cb_ref, alog_ref, dtbias_ref, dsk_ref,
                yf_ref, yb_ref, ext_ref, stf_ref, stb_ref):
    @pl.when(pl.program_id(1) == 0)
    def _():
        stf_ref[...] = jnp.zeros_like(stf_ref)
        stb_ref[...] = jnp.zeros_like(stb_ref)

    _ssd_direction(xf_ref, hf_ref, dtf_ref, cw_ref, cb_ref, alog_ref, dtbias_ref, dsk_ref, ext_ref, stf_ref, yf_ref,
                   reverse=False)
    _ssd_direction(xb_ref, hb_ref, dtb_in_ref, cw_ref, cb_ref, alog_ref, dtbias_ref, dsk_ref, ext_ref, stb_ref, yb_ref,
                   reverse=True)


def _ssd(xbc, halo, dt, conv_w8, conv_b, alog_flat, dtb_flat, dsk_flat):
    b, s, _ = xbc.shape
    nc = s // CHUNK
    fwd = lambda n: pl.BlockSpec((1, CHUNK, n), lambda i, c: (i, c, 0))
    bwd = lambda n: pl.BlockSpec((1, CHUNK, n), lambda i, c: (i, nc - 1 - c, 0))
    hf = pl.BlockSpec((1, 1, 8, CONV_DIM), lambda i, c: (i, c, 0, 0))
    hb = pl.BlockSpec((1, 1, 8, CONV_DIM), lambda i, c: (i, nc - 1 - c, 0, 0))
    const = lambda a: pl.BlockSpec(a.shape, lambda i, c: (0, 0))
    return pl.pallas_call(
        _ssd_kernel,
        grid=(b, nc),
        in_specs=[fwd(CONV_DIM), hf, fwd(LANE), bwd(CONV_DIM), hb, bwd(LANE),
                  const(conv_w8), const(conv_b), const(alog_flat), const(dtb_flat), const(dsk_flat)],
        out_specs=[fwd(D_SSD), bwd(D_SSD)],
        out_shape=[jax.ShapeDtypeStruct((b, s, D_SSD), F32)] * 2,
        scratch_shapes=[
            pltpu.VMEM((CHUNK + 16, CONV_DIM), F32),
            pltpu.VMEM((SSD_HEADS // 2, LANE, LANE), F32),
            pltpu.VMEM((SSD_HEADS // 2, LANE, LANE), F32),
        ],
        compiler_params=_cparams(("parallel", "arbitrary")),
        name="ssd",
    )(xbc, halo, dt, xbc, halo, dt, conv_w8, conv_b, alog_flat, dtb_flat, dsk_flat)


def _outproj_kernel(ya_ref, yf_ref, yb_ref, z_ref, x_ref, g1_ref, sc2_ref, sh2_ref, anw_ref, snw_ref, wout_ref,
                    n2_ref, wr_ref, wrt_ref, x1_ref, h2_ref, afft_ref, aff_ref):
    ya = _rms(ya_ref[0]) * anw_ref[...]
    y = (yf_ref[0] + yb_ref[0]) * _silu(z_ref[0])
    half = D_SSD // SSD_G
    snw = snw_ref[...]
    parts = [ya]
    for g in range(SSD_G):
        sl = slice(g * half, (g + 1) * half)
        parts.append(_rms(y[:, sl]) * snw[:, sl])
    ycat = jnp.concatenate(parts, axis=-1).astype(BF16)
    o = jnp.dot(ycat, wout_ref[...], preferred_element_type=F32)
    x1 = x_ref[0] + g1_ref[0] * o
    x1_ref[0] = x1
    h2 = _rms(x1) * n2_ref[...]
    h2 = h2 * (1.0 + sc2_ref[0]) + sh2_ref[0]
    h2_ref[0] = h2

    lt = lax.dot_general(wrt_ref[...], h2, (((1,), (1,)), ((), ())), precision=HIGHEST,
                         preferred_element_type=F32)
    et = jnp.exp(lt - jnp.max(lt, axis=0, keepdims=True))
    afft_ref[...] = et / jnp.sum(et, axis=0, keepdims=True)
    lg = jnp.dot(h2, wr_ref[...], precision=HIGHEST, preferred_element_type=F32)
    eg = jnp.exp(lg - jnp.max(lg, axis=-1, keepdims=True))
    aff_ref[...] = eg / jnp.sum(eg, axis=-1, keepdims=True)


def _outproj(ya, yf, yb, z, x, g1, sc2, sh2, attn_norm_w, ssd_norm_w, wout16, norm2_w, w_router, tm):
    b, s, d = x.shape
    t = b * s
    nt = s // tm
    tok = lambda n: pl.BlockSpec((1, tm, n), lambda i, j: (i, j, 0))
    per_b = pl.BlockSpec((1, 1, d), lambda i, j: (i, 0, 0))
    vec = lambda n: pl.BlockSpec((1, n), lambda i, j: (0, 0))
    full = lambda a: pl.BlockSpec(a.shape, lambda i, j: (0, 0))
    wrt = w_router.T
    return pl.pallas_call(
        _outproj_kernel,
        grid=(b, nt),
        in_specs=[tok(D_ATTN), tok(D_SSD), tok(D_SSD), tok(D_SSD), tok(d), per_b, per_b, per_b,
                  vec(D_ATTN), vec(D_SSD), full(wout16), vec(d), full(w_router), full(wrt)],
        out_specs=[tok(d), tok(d),
                   pl.BlockSpec((N_EXPERTS, tm), lambda i, j: (0, i * nt + j)),
                   pl.BlockSpec((tm, N_EXPERTS), lambda i, j: (i * nt + j, 0))],
        out_shape=[
            jax.ShapeDtypeStruct((b, s, d), F32),
            jax.ShapeDtypeStruct((b, s, d), F32),
            jax.ShapeDtypeStruct((N_EXPERTS, t), F32),
            jax.ShapeDtypeStruct((t, N_EXPERTS), F32),
        ],
        compiler_params=_cparams(("parallel", "arbitrary")),
        name="outproj",
    )(ya, yf, yb, z, x, g1, sc2, sh2, attn_norm_w.reshape(1, -1), ssd_norm_w.reshape(1, -1), wout16,
      norm2_w.reshape(1, d), w_router, wrt)


def _count(cond):
    c = jnp.sum(jnp.where(cond, 1.0, 0.0), axis=0, keepdims=True)
    return jnp.sum(c, axis=1, keepdims=True)


def _route_kernel(aff_ref, q_ref, idx_ref, starts_ref, *, cap, jc):
    a = aff_ref[0]
    nb = a.shape[0]
    keys = pltpu.bitcast(a, I32)
    kf = jnp.float32(cap)

    tau = jnp.zeros((1, 1), I32)
    for bit in range(30, -1, -1):
        cand = tau | jnp.int32(1 << bit)
        tau = jnp.where(_count(keys >= cand) >= kf, cand, tau)
    gt = keys > tau
    eq = keys == tau
    need = kf - _count(gt)

    r128 = lax.broadcasted_iota(I32, (LANE, LANE), 0)
    c128 = lax.broadcasted_iota(I32, (LANE, LANE), 1)
    upper = (r128 <= c128).astype(BF16)
    ones = jnp.ones((LANE, LANE), BF16)
    rb = lax.broadcasted_iota(I32, (nb, nb), 0)
    cb = lax.broadcasted_iota(I32, (nb, nb), 1)
    lower_strict = (cb < rb).astype(BF16)

    def prefix(m16):
        incl = jnp.dot(m16, upper, preferred_element_type=F32)
        cnt = jnp.dot(m16, ones, preferred_element_type=F32)
        bex = jnp.dot(lower_strict, cnt.astype(BF16), preferred_element_type=F32)
        return bex + incl - m16.astype(F32), cnt, bex

    gt16 = jnp.where(gt, 1.0, 0.0).astype(BF16)
    eq16 = jnp.where(eq, 1.0, 0.0).astype(BF16)
    gt_ex, _, _ = prefix(gt16)
    eq_ex, _, _ = prefix(eq16)
    sel = gt | (eq & (eq_ex < need))
    pos = gt_ex + jnp.minimum(eq_ex, need)
    q_ref[0] = jnp.where(sel, pos, -1.0).astype(I32)

    sel16 = jnp.where(sel, 1.0, 0.0).astype(BF16)
    _, cnt_c, bex_c = prefix(sel16)
    bin_c = bex_c + cnt_c
    lower_incl = (c128 <= r128).astype(BF16)
    incl_t = lax.dot_general(lower_incl, sel16, (((1,), (1,)), ((), ())), preferred_element_type=F32)
    cnt_r = lax.dot_general(jnp.ones((8, LANE), BF16), sel16, (((1,), (1,)), ((), ())),
                            preferred_element_type=F32)
    upper_strict = (rb < cb).astype(BF16)
    bex_r = jnp.dot(cnt_r.astype(BF16), upper_strict, preferred_element_type=F32)
    starts_ref[0] = bex_r[0:1, :].astype(I32)
    g_t = incl_t + bex_r[0:1, :]
    g_hi = jnp.floor(g_t * (1.0 / 64.0))
    g_lo = g_t - 64.0 * g_hi
    g_hi16 = g_hi.astype(BF16)
    g_lo16 = g_lo.astype(BF16)
    reps = jc // LANE
    bex_w = jnp.concatenate([bex_c] * reps, axis=1)
    bin_w = jnp.concatenate([bin_c] * reps, axis=1)
    lane_nb = lax.broadcasted_iota(I32, (nb, jc), 1).astype(F32)
    lane_128 = lax.broadcasted_iota(I32, (LANE, jc), 1).astype(F32)

    def chunk(c, carry):
        base = jnp.asarray(c * jc, I32).astype(F32)
        j_nb = lane_nb + base
        onehot_t = jnp.where((bex_w <= j_nb) & (j_nb < bin_w), 1.0, 0.0).astype(BF16)
        rows = (64.0 * jnp.dot(g_hi16, onehot_t, preferred_element_type=F32)
                + jnp.dot(g_lo16, onehot_t, preferred_element_type=F32))
        offs = jnp.sum(jnp.where(rows <= lane_128 + base, 1.0, 0.0), axis=0, keepdims=True)
        blk = jnp.sum(jnp.where(bin_w <= j_nb, 1.0, 0.0), axis=0, keepdims=True)
        idx_ref[0, :, pl.ds(pl.multiple_of(c * jc, jc), jc)] = (blk * float(LANE) + offs).astype(I32)
        return carry

    lax.fori_loop(0, cap // jc, chunk, 0)


def _route(aff_t, cap):
    e, t = aff_t.shape
    nb = t // LANE
    jc = min(256, cap)
    return pl.pallas_call(
        functools.partial(_route_kernel, cap=cap, jc=jc),
        grid=(e,),
        in_specs=[pl.BlockSpec((1, nb, LANE), lambda i: (i, 0, 0))],
        out_specs=[
            pl.BlockSpec((1, nb, LANE), lambda i: (i, 0, 0)),
            pl.BlockSpec((1, 1, cap), lambda i: (i, 0, 0)),
            pl.BlockSpec((1, 1, nb), lambda i: (i, 0, 0)),
        ],
        out_shape=[
            jax.ShapeDtypeStruct((e, nb, LANE), I32),
            jax.ShapeDtypeStruct((e, 1, cap), I32),
            jax.ShapeDtypeStruct((e, 1, nb), I32),
        ],
        compiler_params=_cparams(("arbitrary",)),
        name="route",
    )(aff_t.reshape(e, nb, LANE))


def _expert_kernel(idx_hbm, h_hbm, wg_ref, wu_ref, wd_ref, ye_ref, idx_smem, xbuf, sem_idx, sem_rows, *, rm, nj, fc):
    e = pl.program_id(0)
    j = pl.program_id(1)
    step = e * nj + j
    nsteps = pl.num_programs(0) * nj
    slot = step % 2

    def idx_copy(st, sl):
        return pltpu.make_async_copy(idx_hbm.at[st], idx_smem.at[sl], sem_idx.at[sl])

    def gather(st, sl):
        cp = idx_copy(st, sl)
        cp.start()
        cp.wait()

        def issue(r, carry):
            tok = idx_smem[sl, 0, r]
            pltpu.make_async_copy(h_hbm.at[pl.ds(tok, 1)], xbuf.at[sl, pl.ds(r, 1)], sem_rows.at[sl]).start()
            return carry

        lax.fori_loop(0, rm, issue, 0)

    @pl.when(step == 0)
    def _():
        gather(0, 0)

    @pl.when(step + 1 < nsteps)
    def _():
        gather(step + 1, 1 - slot)

    pltpu.make_async_copy(h_hbm.at[pl.ds(0, rm)], xbuf.at[slot], sem_rows.at[slot]).wait()

    x = xbuf[slot].astype(BF16)
    acc = jnp.zeros((rm, D_MODEL), F32)
    for c in range(D_EXPERT // fc):
        sl = slice(c * fc, (c + 1) * fc)
        g = jnp.dot(x, wg_ref[0, :, sl], preferred_element_type=F32)
        u = jnp.dot(x, wu_ref[0, :, sl], preferred_element_type=F32)
        hid = (_silu(g) * u).astype(BF16)
        acc = acc + jnp.dot(hid, wd_ref[0, sl, :], preferred_element_type=F32)
    ye_ref[...] = acc.astype(BF16)


def _experts(idx, h2, wg16, wu16, wd16, rm):
    e, _, cap = idx.shape
    nj = cap // rm
    t, d = h2.shape
    idx_steps = idx.reshape(e * nj, 1, rm)
    return pl.pallas_call(
        functools.partial(_expert_kernel, rm=rm, nj=nj, fc=512),
        grid=(e, nj),
        in_specs=[
            pl.BlockSpec(memory_space=pl.ANY),
            pl.BlockSpec(memory_space=pl.ANY),
            pl.BlockSpec((1, d, D_EXPERT), lambda i, j: (i, 0, 0)),
            pl.BlockSpec((1, d, D_EXPERT), lambda i, j: (i, 0, 0)),
            pl.BlockSpec((1, D_EXPERT, d), lambda i, j: (i, 0, 0)),
        ],
        out_specs=pl.BlockSpec((rm, d), lambda i, j: (i * nj + j, 0)),
        out_shape=jax.ShapeDtypeStruct((e * cap, d), BF16),
        scratch_shapes=[
            pltpu.SMEM((2, 1, rm), I32),
            pltpu.VMEM((2, rm, d), F32),
            pltpu.SemaphoreType.DMA((2,)),
            pltpu.SemaphoreType.DMA((2,)),
        ],
        compiler_params=_cparams(("arbitrary", "arbitrary")),
        name="expert",
    )(idx_steps, h2, wg16, wu16, wd16)


def _combine_kernel(starts_ref, q_ref, aff_ref, x1_ref, g2_ref, fnw_ref, ye_hbm, y_ref, ybuf, sem, *, cap, win, tm):
    i = pl.program_id(0)
    n = pl.num_programs(0)
    slot = i % 2
    total = N_EXPERTS * cap

    def window_start(tile, e):
        gs = e * cap + starts_ref[tile * N_EXPERTS + e]
        a = jnp.minimum((gs // BF16_ROWS) * BF16_ROWS, total - win)
        return pl.multiple_of(a, BF16_ROWS)

    def copies(tile, sl):
        return [pltpu.make_async_copy(ye_hbm.at[pl.ds(window_start(tile, e), win)], ybuf.at[sl, e], sem.at[sl])
                for e in range(N_EXPERTS)]

    @pl.when(i == 0)
    def _():
        for cp in copies(0, 0):
            cp.start()

    @pl.when(i + 1 < n)
    def _():
        for cp in copies(i + 1, 1 - slot):
            cp.start()

    for cp in copies(i, slot):
        cp.wait()

    q = q_ref[...]
    aff = aff_ref[...]
    w_iota = lax.broadcasted_iota(I32, (tm, win), 1)
    acc = jnp.zeros((tm, D_MODEL), F32)
    for e in range(N_EXPERTS):
        qe = q[:, e:e + 1]
        rel = qe + (e * cap - window_start(i, e))
        onehot = jnp.where((qe >= 0) & (rel == w_iota), 1.0, 0.0).astype(BF16)
        part = jnp.dot(onehot, ybuf[slot, e], preferred_element_type=F32)
        acc = acc + aff[:, e:e + 1] * part
    out = x1_ref[...] + g2_ref[0] * acc
    y_ref[...] = _rms(out) * fnw_ref[...]


def _combine(starts_flat, q_tok, aff, x1, g2, final_norm_w, ye, cap, tm, tiles_per_batch):
    t, d = x1.shape
    win = tm + BF16_ROWS
    grid_spec = pltpu.PrefetchScalarGridSpec(
        num_scalar_prefetch=1,
        grid=(t // tm,),
        in_specs=[
            pl.BlockSpec((tm, N_EXPERTS), lambda i, s: (i, 0)),
            pl.BlockSpec((tm, N_EXPERTS), lambda i, s: (i, 0)),
            pl.BlockSpec((tm, d), lambda i, s: (i, 0)),
            pl.BlockSpec((1, 1, d), lambda i, s: (i // tiles_per_batch, 0, 0)),
            pl.BlockSpec((1, d), lambda i, s: (0, 0)),
            pl.BlockSpec(memory_space=pl.ANY),
        ],
        out_specs=pl.BlockSpec((tm, d), lambda i, s: (i, 0)),
        scratch_shapes=[
            pltpu.VMEM((2, N_EXPERTS, win, d), BF16),
            pltpu.SemaphoreType.DMA((2,)),
        ],
    )
    return pl.pallas_call(
        functools.partial(_combine_kernel, cap=cap, win=win, tm=tm),
        grid_spec=grid_spec,
        out_shape=jax.ShapeDtypeStruct((t, d), F32),
        compiler_params=_cparams(("arbitrary",)),
        name="combine",
    )(starts_flat, q_tok, aff, x1, g2, final_norm_w.reshape(1, d), ye)


def _rotate_half_cols(w):
    half = ROPE // 2
    return jnp.concatenate([-w[..., half:], w[..., :half]], axis=-1)


def _prep_weights(w_in, w_uq, w_ukv, w_out, w_gate, w_up, w_down):
    d = w_in.shape[0]
    offs = [Q_RANK, Q_RANK + KV_RANK, Q_RANK + KV_RANK + ROPE, Q_RANK + KV_RANK + ROPE + D_SSD,
            Q_RANK + KV_RANK + ROPE + D_SSD + CONV_DIM]
    w_cq, w_ckv, w_kr, w_z, w_xbc, w_dt = jnp.split(w_in, offs, axis=1)
    place = lambda w: jnp.pad(w, ((0, 0), (NOPE, HEAD_PAD - NOPE - ROPE)))
    win_ext = jnp.concatenate(
        [w_cq, w_ckv, place(w_kr), place(_rotate_half_cols(w_kr)), w_z, w_xbc,
         jnp.pad(w_dt, ((0, 0), (0, LANE - w_dt.shape[1])))], axis=1).astype(BF16)
    assert win_ext.shape == (d, _N_IN)

    wq = w_uq.reshape(Q_RANK, N_HEADS, NOPE + ROPE)
    pad_h = lambda w: jnp.pad(w, ((0, 0), (0, 0), (0, HEAD_PAD - NOPE - ROPE))).reshape(Q_RANK, N_HEADS * HEAD_PAD)
    wq_rot = jnp.concatenate([jnp.zeros_like(wq[..., :NOPE]), _rotate_half_cols(wq[..., NOPE:])], axis=-1)
    wq_ext = jnp.concatenate([pad_h(wq), pad_h(wq_rot)], axis=1).astype(BF16)

    wkv = w_ukv.reshape(KV_RANK, N_HEADS, NOPE + V_DIM)
    wk = jnp.pad(wkv[..., :NOPE], ((0, 0), (0, 0), (0, HEAD_PAD - NOPE))).reshape(KV_RANK, N_HEADS * HEAD_PAD)
    wv = wkv[..., NOPE:].reshape(KV_RANK, D_ATTN)
    wkv_ext = jnp.concatenate([wk, wv], axis=1).astype(BF16)
    return win_ext, wq_ext, wkv_ext, w_out.astype(BF16), w_gate.astype(BF16), w_up.astype(BF16), w_down.astype(BF16)


def _rope_table(seq):
    pos = jnp.arange(seq, dtype=F32)
    inv = 1.0 / (ROPE_THETA ** (jnp.arange(0, ROPE, 2, dtype=F32) / ROPE))
    ang = pos[:, None] * inv[None, :]
    cos = jnp.concatenate([jnp.cos(ang)] * 2, axis=-1)
    sin = jnp.concatenate([jnp.sin(ang)] * 2, axis=-1)
    zpad = jnp.zeros((seq, HEAD_PAD - NOPE - ROPE), F32)
    qs = (NOPE + ROPE) ** -0.5 * math.log2(math.e)
    cosq = jnp.concatenate([jnp.ones((seq, NOPE), F32), cos, zpad], axis=-1) * qs
    sinq = jnp.concatenate([jnp.zeros((seq, NOPE), F32), sin, zpad], axis=-1) * qs
    cosk = jnp.concatenate([jnp.zeros((seq, NOPE), F32), cos, zpad], axis=-1)
    sink = jnp.concatenate([jnp.zeros((seq, NOPE), F32), sin, zpad], axis=-1)
    return jnp.concatenate([cosq, sinq, cosk, sink], axis=-1)


def _conv_halo(xbc):
    b, s, c = xbc.shape
    nc = s // CHUNK
    r = xbc.reshape(b, nc, CHUNK, c)
    z2 = jnp.zeros((b, 1, 2, c), xbc.dtype)
    prev = jnp.concatenate([z2, r[:, :-1, CHUNK - 2:, :]], axis=1)
    nxt = jnp.concatenate([r[:, 1:, :2, :], z2], axis=1)
    return jnp.concatenate([prev, nxt, jnp.zeros((b, nc, 4, c), xbc.dtype)], axis=2)


def _trunk(x, mod, p):
    b, s, d = x.shape
    t = b * s
    cap = max(1, CAP_FACTOR * t // N_EXPERTS)
    sh1, sc1, g1, sh2, sc2, g2 = [m.reshape(b, 1, d) for m in jnp.split(mod, 6, axis=-1)]

    tm = min(256, s)
    q, k, v, z, xbc, dt = _inproj(x, sc1, sh1, p["norm1_w"], p["win_ext"], p["q_norm_w"], p["wq_ext"],
                                  p["kv_norm_w"], p["wkv_ext"], _rope_table(s), tm)
    ya = _attention(q, k, v, min(256, s))
    yf, yb = _ssd(xbc, _conv_halo(xbc), dt, p["conv_w8"], p["conv_b"], p["alog_flat"], p["dtb_flat"], p["dsk_flat"])
    x1, h2, aff_t, aff = _outproj(ya, yf, yb, z, x, g1, sc2, sh2, p["attn_norm_w"], p["ssd_norm_w"], p["wout16"],
                                  p["norm2_w"], p["w_router"], tm)
    q_slot, idx, starts = _route(aff_t, cap)
    ye = _experts(idx, h2.reshape(t, d), p["wg16"], p["wu16"], p["wd16"], min(256, cap))
    q_tok = q_slot.reshape(N_EXPERTS, t).T
    starts_flat = starts.reshape(N_EXPERTS, t // LANE).T.reshape(-1)
    y = _combine(starts_flat, q_tok, aff, x1.reshape(t, d), g2, p["final_norm_w"], ye, cap, LANE, s // LANE)
    return y.reshape(b, s, d)


def kernel(x_prompt, x_sample, c_prompt, c_sample, w_ada, b_ada, norm1_w, w_in, q_norm_w, w_uq, kv_norm_w, w_ukv,
           attn_norm_w, conv_w, conv_b, a_log, dt_bias, d_skip, ssd_norm_w, w_out, norm2_w, w_router, w_gate, w_up,
           w_down, final_norm_w):
    win_ext, wq_ext, wkv_ext, wout16, wg16, wu16, wd16 = _prep_weights(
        w_in[0], w_uq[0], w_ukv[0], w_out[0], w_gate[0], w_up[0], w_down[0])
    flat16 = lambda a: jnp.pad(a.reshape(1, 2 * SSD_HEADS), ((0, 0), (0, LANE - 2 * SSD_HEADS)))
    p = dict(
        norm1_w=norm1_w[0], win_ext=win_ext, q_norm_w=q_norm_w[0], wq_ext=wq_ext, kv_norm_w=kv_norm_w[0],
        wkv_ext=wkv_ext, attn_norm_w=attn_norm_w[0], ssd_norm_w=ssd_norm_w[0], wout16=wout16, norm2_w=norm2_w[0],
        w_router=w_router[0], wg16=wg16, wu16=wu16, wd16=wd16, final_norm_w=final_norm_w,
        conv_w8=jnp.pad(conv_w[0], ((0, 8 - D_CONV), (0, 0))), conv_b=conv_b[0].reshape(1, CONV_DIM),
        alog_flat=flat16(a_log[0]), dtb_flat=flat16(dt_bias[0]),
        dsk_flat=jnp.repeat(d_skip[0], SSD_P).reshape(1, D_SSD),
    )
    nbp = c_prompt.shape[0]
    mod = _mod(jnp.concatenate([c_prompt, c_sample], axis=0), w_ada[0], b_ada[0])
    return _trunk(x_prompt, mod[:nbp], p), _trunk(x_sample, mod[nbp:], p)
```

```python
import functools
import math

import jax
import jax.numpy as jnp
from jax import lax
from jax.experimental import pallas as pl
from jax.experimental.pallas import tpu as pltpu

F32 = jnp.float32
BF16 = jnp.bfloat16
I32 = jnp.int32
HIGHEST = lax.Precision.HIGHEST

D_MODEL = 1024
N_HEADS = 8
NOPE = 64
ROPE = 32
V_DIM = 64
Q_RANK = 256
KV_RANK = 128
ROPE_THETA = 10000.0
D_ATTN = N_HEADS * V_DIM
SSD_HEADS = 8
SSD_P = 64
D_SSD = SSD_HEADS * SSD_P
SSD_G = 2
SSD_N = 64
D_CONV = 5
CHUNK = 128
CONV_DIM = D_SSD + 2 * SSD_G * SSD_N
N_EXPERTS = 16
CAP_FACTOR = 2
D_EXPERT = 2048
EPS = 1e-6

LANE = 128
HEAD_PAD = 128
BF16_ROWS = 16
VMEM_LIMIT = 56 * 1024 * 1024
_SCORE_ELEMS = 1 << 20

_C_CQ = 0
_C_CKV = _C_CQ + Q_RANK
_C_KR = _C_CKV + KV_RANK
_C_KRR = _C_KR + LANE
_C_Z = _C_KRR + LANE
_C_XBC = _C_Z + D_SSD
_C_DT = _C_XBC + CONV_DIM
_N_IN = _C_DT + LANE


def _cparams(sem, vmem=VMEM_LIMIT):
    return pltpu.CompilerParams(dimension_semantics=sem, vmem_limit_bytes=vmem)


def _silu(x):
    return x / (1.0 + jnp.exp(-x))


def _rms(x):
    return x * lax.rsqrt(jnp.mean(x * x, axis=-1, keepdims=True) + EPS)


def _mod_kernel(c_ref, w_ref, b_ref, o_ref):
    a = _silu(c_ref[...])
    o_ref[...] = jnp.dot(a, w_ref[...], precision=HIGHEST, preferred_element_type=F32) + b_ref[...]


def _mod(c, w_ada, b_ada):
    nb, d = c.shape
    n = w_ada.shape[1]
    tn = 1536
    return pl.pallas_call(
        _mod_kernel,
        grid=(n // tn,),
        in_specs=[
            pl.BlockSpec((nb, d), lambda j: (0, 0)),
            pl.BlockSpec((d, tn), lambda j: (0, j)),
            pl.BlockSpec((1, tn), lambda j: (0, j)),
        ],
        out_specs=pl.BlockSpec((nb, tn), lambda j: (0, j)),
        out_shape=jax.ShapeDtypeStruct((nb, n), F32),
        compiler_params=_cparams(("arbitrary",)),
        name="mod",
    )(c, w_ada, b_ada.reshape(1, n))


def _inproj_kernel(x_ref, sc_ref, sh_ref, n1_ref, win_ref, qn_ref, wq_ref, kvn_ref, wkv_ref, tab_ref,
                   q_ref, k_ref, v_ref, z_ref, xbc_ref, dt_ref):
    h = _rms(x_ref[0]) * n1_ref[...]
    h = h * (1.0 + sc_ref[0]) + sh_ref[0]
    proj = jnp.dot(h.astype(BF16), win_ref[...], preferred_element_type=F32)
    z_ref[0] = proj[:, _C_Z:_C_XBC]
    xbc_ref[0] = proj[:, _C_XBC:_C_DT]
    dt_ref[0] = proj[:, _C_DT:_N_IN]

    tab = tab_ref[...]
    cosq, sinq = tab[:, 0:LANE], tab[:, LANE:2 * LANE]
    cosk, sink = tab[:, 2 * LANE:3 * LANE], tab[:, 3 * LANE:4 * LANE]

    cqn = (_rms(proj[:, _C_CQ:_C_CKV]) * qn_ref[...]).astype(BF16)
    q2 = jnp.dot(cqn, wq_ref[...], preferred_element_type=F32)
    ckvn = (_rms(proj[:, _C_CKV:_C_KR]) * kvn_ref[...]).astype(BF16)
    kv = jnp.dot(ckvn, wkv_ref[...], preferred_element_type=F32)
    krope = proj[:, _C_KR:_C_KRR] * cosk + proj[:, _C_KRR:_C_Z] * sink
    nq = N_HEADS * HEAD_PAD
    is_v = lax.broadcasted_iota(I32, (x_ref.shape[1], HEAD_PAD), 1) < V_DIM
    for hd in range(N_HEADS):
        sl = slice(hd * HEAD_PAD, (hd + 1) * HEAD_PAD)
        rot = slice(nq + hd * HEAD_PAD, nq + (hd + 1) * HEAD_PAD)
        q_ref[0, hd] = (q2[:, sl] * cosq + q2[:, rot] * sinq).astype(BF16)
        k_ref[0, hd] = (kv[:, sl] + krope).astype(BF16)
        v_ref[0, hd] = jnp.where(is_v, kv[:, rot], 1.0).astype(BF16)


def _inproj(x, sc1, sh1, norm1_w, win_ext, q_norm_w, wq_ext, kv_norm_w, wkv_ext, tab, tm):
    b, s, d = x.shape
    vec = lambda n: pl.BlockSpec((1, n), lambda i, j: (0, 0))
    full = lambda a: pl.BlockSpec(a.shape, lambda i, j: (0, 0))
    per_b = pl.BlockSpec((1, 1, d), lambda i, j: (i, 0, 0))
    tok = lambda n: pl.BlockSpec((1, tm, n), lambda i, j: (i, j, 0))
    heads = pl.BlockSpec((1, N_HEADS, tm, HEAD_PAD), lambda i, j: (i, 0, j, 0))
    head_shape = jax.ShapeDtypeStruct((b, N_HEADS, s, HEAD_PAD), BF16)
    return pl.pallas_call(
        _inproj_kernel,
        grid=(b, s // tm),
        in_specs=[tok(d), per_b, per_b, vec(d), full(win_ext), vec(Q_RANK), full(wq_ext), vec(KV_RANK),
                  full(wkv_ext), pl.BlockSpec((tm, 4 * LANE), lambda i, j: (j, 0))],
        out_specs=[heads, heads, heads, tok(D_SSD), tok(CONV_DIM), tok(LANE)],
        out_shape=[
            head_shape, head_shape, head_shape,
            jax.ShapeDtypeStruct((b, s, D_SSD), F32),
            jax.ShapeDtypeStruct((b, s, CONV_DIM), F32),
            jax.ShapeDtypeStruct((b, s, LANE), F32),
        ],
        compiler_params=_cparams(("parallel", "arbitrary")),
        name="inproj",
    )(x, sc1, sh1, norm1_w.reshape(1, d), win_ext, q_norm_w.reshape(1, -1), wq_ext,
      kv_norm_w.reshape(1, -1), wkv_ext, tab)


def _attn_kernel(q_ref, k_ref, v_ref, o_ref, s_scr):
    lane = lax.broadcasted_iota(I32, (q_ref.shape[2], HEAD_PAD), 1)
    first = lane < V_DIM

    def scores(h):
        s = lax.dot_general(q_ref[0, h], k_ref[0, h], (((1,), (1,)), ((), ())), preferred_element_type=F32)
        s_scr[h % 2] = s
        return jnp.max(s, axis=-1, keepdims=True)

    def finish(h, m):
        p = jnp.exp2(s_scr[h % 2] - m).astype(BF16)
        o = jnp.dot(p, v_ref[0, h], preferred_element_type=F32)
        return o / pltpu.roll(o, V_DIM, axis=1)

    def emit(h, o, prev):
        if h % 2 == 0:
            return o
        pair = h // 2
        o_ref[0, :, pair * HEAD_PAD:(pair + 1) * HEAD_PAD] = jnp.where(first, prev, pltpu.roll(o, V_DIM, axis=1))
        return None

    m_prev = scores(0)
    held = None
    for h in range(1, N_HEADS):
        m = scores(h)
        held = emit(h - 1, finish(h - 1, m_prev), held)
        m_prev = m
    emit(N_HEADS - 1, finish(N_HEADS - 1, m_prev), held)


def _attention(q, k, v, tq):
    b, _, s, _ = q.shape
    kv_spec = pl.BlockSpec((1, N_HEADS, s, HEAD_PAD), lambda i, j: (i, 0, 0, 0))
    return pl.pallas_call(
        _attn_kernel,
        grid=(b, s // tq),
        in_specs=[pl.BlockSpec((1, N_HEADS, tq, HEAD_PAD), lambda i, j: (i, 0, j, 0)), kv_spec, kv_spec],
        out_specs=pl.BlockSpec((1, tq, D_ATTN), lambda i, j: (i, j, 0)),
        out_shape=jax.ShapeDtypeStruct((b, s, D_ATTN), F32),
        scratch_shapes=[pltpu.VMEM((2, tq, s), F32)],
        compiler_params=_cparams(("parallel", "arbitrary")),
        name="attn",
    )(q, k, v)


def _ssd_direction(xbc_ref, halo_ref, dt_ref, cw_ref, cb_ref, alog_ref, dtb_ref, dsk_ref, ext_ref, st_ref, y_ref,
                   reverse):
    q = CHUNK
    ext_ref[6:8, :] = halo_ref[0, 0, 0:2, :]
    ext_ref[8:8 + q, :] = xbc_ref[0]
    ext_ref[8 + q:10 + q, :] = halo_ref[0, 0, 2:4, :]
    conv = cb_ref[...] + cw_ref[0:1, :] * ext_ref[6:6 + q, :]
    for kk in range(1, D_CONV):
        conv = conv + cw_ref[kk:kk + 1, :] * ext_ref[6 + kk:6 + kk + q, :]
    act = _silu(conv)
    bm = act[:, D_SSD:D_SSD + LANE]
    cm = act[:, D_SSD + LANE:D_SSD + 2 * LANE]

    off = SSD_HEADS if reverse else 0
    raw = dt_ref[0] + dtb_ref[...]
    dt = jnp.maximum(raw, 0.0) + jnp.log1p(jnp.exp(-jnp.abs(raw)))
    dta = dt * (-jnp.exp(alog_ref[...]))
    ri = lax.broadcasted_iota(I32, (q, q), 0)
    ci = lax.broadcasted_iota(I32, (q, q), 1)
    mask = (ci >= ri) if reverse else (ci <= ri)
    cs = jnp.dot(mask.astype(F32), dta, precision=HIGHEST, preferred_element_type=F32)
    cs_t = cs.T
    dt_t = dt.T
    end = 0 if reverse else q - 1
    cs_end = cs[end:end + 1, :]
    wdec = jnp.exp(cs_end - cs) * dt
    ecs = jnp.exp(cs)
    cdec = jnp.exp(cs_end)

    lane = lax.broadcasted_iota(I32, (q, LANE), 1)
    sub = lax.broadcasted_iota(I32, (LANE, LANE), 0)
    cm16 = cm.astype(BF16)
    cbs = []
    for g in range(SSD_G):
        in_g = (lane >= g * SSD_N) & (lane < (g + 1) * SSD_N)
        cg = jnp.where(in_g, cm, 0.0).astype(BF16)
        cbs.append(lax.dot_general(cg, bm.astype(BF16), (((1,), (1,)), ((), ())), preferred_element_type=F32))

    for j in range(SSD_HEADS // 2):
        g = (2 * j) // (SSD_HEADS // SSD_G)
        xp = act[:, j * LANE:(j + 1) * LANE]
        xp16 = xp.astype(BF16)
        yd = []
        for hh in (2 * j, 2 * j + 1):
            ln = off + hh
            seg = cs[:, ln:ln + 1] - cs_t[ln:ln + 1, :]
            dec = jnp.exp(jnp.where(mask, seg, -jnp.inf))
            m = (cbs[g] * dec * dt_t[ln:ln + 1, :]).astype(BF16)
            yd.append(jnp.dot(m, xp16, preferred_element_type=F32))
        la, lb = off + 2 * j, off + 2 * j + 1
        first = lane < SSD_P
        y = jnp.where(first, yd[0], yd[1])
        st = st_ref[j]
        y_off = lax.dot_general(cm16, st.astype(BF16), (((1,), (1,)), ((), ())), preferred_element_type=F32)
        y = y + y_off * jnp.where(first, ecs[:, la:la + 1], ecs[:, lb:lb + 1])
        xw = (xp * jnp.where(first, wdec[:, la:la + 1], wdec[:, lb:lb + 1])).astype(BF16)
        in_g = (lane >= g * SSD_N) & (lane < (g + 1) * SSD_N)
        bg = jnp.where(in_g, bm, 0.0).astype(BF16)
        new = lax.dot_general(xw, bg, (((0,), (0,)), ((), ())), preferred_element_type=F32)
        st_ref[j] = st * jnp.where(sub < SSD_P, cdec[:, la:la + 1], cdec[:, lb:lb + 1]) + new
        if not reverse:
            y = y + dsk_ref[:, j * LANE:(j + 1) * LANE] * xp
        y_ref[0, :, j * LANE:(j + 1) * LANE] = y


def _ssd_kernel(xf_ref, hf_ref, dtf_ref, xb_ref, hb_ref, dtb_in_ref, cw_ref, cb_ref, alog_ref, dtbias_ref, dsk_ref,
                yf_ref, yb_ref, ext_ref, stf_ref, stb_ref):
    @pl.when(pl.program_id(1) == 0)
    def _():
        stf_ref[...] = jnp.zeros_like(stf_ref)
        stb_ref[...] = jnp.zeros_like(stb_ref)

    _ssd_direction(xf_ref, hf_ref, dtf_ref, cw_ref, cb_ref, alog_ref, dtbias_ref, dsk_ref, ext_ref, stf_ref, yf_ref,
                   reverse=False)
    _ssd_direction(xb_ref, hb_ref, dtb_in_ref, cw_ref, cb_ref, alog_ref, dtbias_ref, dsk_ref, ext_ref, stb_ref, yb_ref,
                   reverse=True)


def _ssd(xbc, halo, dt, conv_w8, conv_b, alog_flat, dtb_flat, dsk_flat):
    b, s, _ = xbc.shape
    nc = s // CHUNK
    fwd = lambda n: pl.BlockSpec((1, CHUNK, n), lambda i, c: (i, c, 0))
    bwd = lambda n: pl.BlockSpec((1, CHUNK, n), lambda i, c: (i, nc - 1 - c, 0))
    hf = pl.BlockSpec((1, 1, 8, CONV_DIM), lambda i, c: (i, c, 0, 0))
    hb = pl.BlockSpec((1, 1, 8, CONV_DIM), lambda i, c: (i, nc - 1 - c, 0, 0))
    const = lambda a: pl.BlockSpec(a.shape, lambda i, c: (0, 0))
    return pl.pallas_call(
        _ssd_kernel,
        grid=(b, nc),
        in_specs=[fwd(CONV_DIM), hf, fwd(LANE), bwd(CONV_DIM), hb, bwd(LANE),
                  const(conv_w8), const(conv_b), const(alog_flat), const(dtb_flat), const(dsk_flat)],
        out_specs=[fwd(D_SSD), bwd(D_SSD)],
        out_shape=[jax.ShapeDtypeStruct((b, s, D_SSD), F32)] * 2,
        scratch_shapes=[
            pltpu.VMEM((CHUNK + 16, CONV_DIM), F32),
            pltpu.VMEM((SSD_HEADS // 2, LANE, LANE), F32),
            pltpu.VMEM((SSD_HEADS // 2, LANE, LANE), F32),
        ],
        compiler_params=_cparams(("parallel", "arbitrary")),
        name="ssd",
    )(xbc, halo, dt, xbc, halo, dt, conv_w8, conv_b, alog_flat, dtb_flat, dsk_flat)


def _outproj_kernel(ya_ref, yf_ref, yb_ref, z_ref, x_ref, g1_ref, sc2_ref, sh2_ref, anw_ref, snw_ref, wout_ref,
                    n2_ref, wrt_ref, x1_ref, h2_ref, afft_ref, aff_ref):
    ya = _rms(ya_ref[0]) * anw_ref[...]
    y = (yf_ref[0] + yb_ref[0]) * _silu(z_ref[0])
    half = D_SSD // SSD_G
    snw = snw_ref[...]
    parts = [ya]
    for g in range(SSD_G):
        sl = slice(g * half, (g + 1) * half)
        parts.append(_rms(y[:, sl]) * snw[:, sl])
    ycat = jnp.concatenate(parts, axis=-1).astype(BF16)
    o = jnp.dot(ycat, wout_ref[...], preferred_element_type=F32)
    x1 = x_ref[0] + g1_ref[0] * o
    x1_ref[0] = x1
    h2 = _rms(x1) * n2_ref[...]
    h2 = h2 * (1.0 + sc2_ref[0]) + sh2_ref[0]
    h2_ref[0] = h2

    lt = lax.dot_general(wrt_ref[...], h2, (((1,), (1,)), ((), ())), precision=HIGHEST,
                         preferred_element_type=F32)
    et = jnp.exp(lt - jnp.max(lt, axis=0, keepdims=True))
    aff_t = et / jnp.sum(et, axis=0, keepdims=True)
    afft_ref[...] = aff_t
    padded = jnp.concatenate([aff_t, jnp.zeros((LANE - N_EXPERTS, aff_t.shape[1]), F32)], axis=0)
    aff_ref[...] = padded.T[:, :N_EXPERTS]


def _outproj(ya, yf, yb, z, x, g1, sc2, sh2, attn_norm_w, ssd_norm_w, wout16, norm2_w, w_router, tm):
    b, s, d = x.shape
    t = b * s
    nt = s // tm
    tok = lambda n: pl.BlockSpec((1, tm, n), lambda i, j: (i, j, 0))
    per_b = pl.BlockSpec((1, 1, d), lambda i, j: (i, 0, 0))
    vec = lambda n: pl.BlockSpec((1, n), lambda i, j: (0, 0))
    full = lambda a: pl.BlockSpec(a.shape, lambda i, j: (0, 0))
    wrt = w_router.T
    return pl.pallas_call(
        _outproj_kernel,
        grid=(b, nt),
        in_specs=[tok(D_ATTN), tok(D_SSD), tok(D_SSD), tok(D_SSD), tok(d), per_b, per_b, per_b,
                  vec(D_ATTN), vec(D_SSD), full(wout16), vec(d), full(wrt)],
        out_specs=[tok(d), tok(d),
                   pl.BlockSpec((N_EXPERTS, tm), lambda i, j: (0, i * nt + j)),
                   pl.BlockSpec((tm, N_EXPERTS), lambda i, j: (i * nt + j, 0))],
        out_shape=[
            jax.ShapeDtypeStruct((b, s, d), F32),
            jax.ShapeDtypeStruct((b, s, d), F32),
            jax.ShapeDtypeStruct((N_EXPERTS, t), F32),
            jax.ShapeDtypeStruct((t, N_EXPERTS), F32),
        ],
        compiler_params=_cparams(("parallel", "arbitrary")),
        name="outproj",
    )(ya, yf, yb, z, x, g1, sc2, sh2, attn_norm_w.reshape(1, -1), ssd_norm_w.reshape(1, -1), wout16,
      norm2_w.reshape(1, d), wrt)


def _count(cond):
    c = jnp.sum(jnp.where(cond, 1.0, 0.0), axis=0, keepdims=True)
    return jnp.sum(c, axis=1, keepdims=True)


def _route_kernel(aff_ref, q_ref, idx_ref, starts_ref, *, cap, jc):
    a = aff_ref[0]
    nb = a.shape[0]
    keys = pltpu.bitcast(a, I32)
    kf = jnp.float32(cap)

    tau = jnp.zeros((1, 1), I32)
    for bit in range(30, -1, -1):
        cand = tau | jnp.int32(1 << bit)
        tau = jnp.where(_count(keys >= cand) >= kf, cand, tau)
    gt = keys > tau
    eq = keys == tau
    need = kf - _count(gt)

    r128 = lax.broadcasted_iota(I32, (LANE, LANE), 0)
    c128 = lax.broadcasted_iota(I32, (LANE, LANE), 1)
    upper = (r128 <= c128).astype(BF16)
    ones = jnp.ones((LANE, LANE), BF16)
    rb = lax.broadcasted_iota(I32, (nb, nb), 0)
    cb = lax.broadcasted_iota(I32, (nb, nb), 1)
    lower_strict = (cb < rb).astype(BF16)

    def prefix(m16):
        incl = jnp.dot(m16, upper, preferred_element_type=F32)
        cnt = jnp.dot(m16, ones, preferred_element_type=F32)
        bex = jnp.dot(lower_strict, cnt.astype(BF16), preferred_element_type=F32)
        return bex + incl - m16.astype(F32), cnt, bex

    gt16 = jnp.where(gt, 1.0, 0.0).astype(BF16)
    eq16 = jnp.where(eq, 1.0, 0.0).astype(BF16)
    gt_ex, _, _ = prefix(gt16)
    eq_ex, _, _ = prefix(eq16)
    sel = gt | (eq & (eq_ex < need))
    pos = gt_ex + jnp.minimum(eq_ex, need)
    q_ref[0] = jnp.where(sel, pos, -1.0).astype(I32)

    sel16 = jnp.where(sel, 1.0, 0.0).astype(BF16)
    _, cnt_c, bex_c = prefix(sel16)
    bin_c = bex_c + cnt_c
    lower_incl = (c128 <= r128).astype(BF16)
    incl_t = lax.dot_general(lower_incl, sel16, (((1,), (1,)), ((), ())), preferred_element_type=F32)
    cnt_r = lax.dot_general(jnp.ones((8, LANE), BF16), sel16, (((1,), (1,)), ((), ())),
                            preferred_element_type=F32)
    upper_strict = (rb < cb).astype(BF16)
    bex_r = jnp.dot(cnt_r.astype(BF16), upper_strict, preferred_element_type=F32)
    starts_ref[0] = bex_r[0:1, :].astype(I32)
    g_t = incl_t + bex_r[0:1, :]
    g_hi = jnp.floor(g_t * (1.0 / 64.0))
    g_lo = g_t - 64.0 * g_hi
    g_hi16 = g_hi.astype(BF16)
    g_lo16 = g_lo.astype(BF16)
    reps = jc // LANE
    bex_w = jnp.concatenate([bex_c] * reps, axis=1)
    bin_w = jnp.concatenate([bin_c] * reps, axis=1)
    lane_nb = lax.broadcasted_iota(I32, (nb, jc), 1).astype(F32)
    lane_128 = lax.broadcasted_iota(I32, (LANE, jc), 1).astype(F32)

    def chunk(c, carry):
        base = jnp.asarray(c * jc, I32).astype(F32)
        j_nb = lane_nb + base
        onehot_t = jnp.where((bex_w <= j_nb) & (j_nb < bin_w), 1.0, 0.0).astype(BF16)
        rows = (64.0 * jnp.dot(g_hi16, onehot_t, preferred_element_type=F32)
                + jnp.dot(g_lo16, onehot_t, preferred_element_type=F32))
        offs = jnp.sum(jnp.where(rows <= lane_128 + base, 1.0, 0.0), axis=0, keepdims=True)
        blk = jnp.sum(jnp.where(bin_w <= j_nb, 1.0, 0.0), axis=0, keepdims=True)
        idx_ref[0, :, pl.ds(pl.multiple_of(c * jc, jc), jc)] = (blk * float(LANE) + offs).astype(I32)
        return carry

    lax.fori_loop(0, cap // jc, chunk, 0)


def _route(aff_t, cap):
    e, t = aff_t.shape
    nb = t // LANE
    jc = min(256, cap)
    return pl.pallas_call(
        functools.partial(_route_kernel, cap=cap, jc=jc),
        grid=(e,),
        in_specs=[pl.BlockSpec((1, nb, LANE), lambda i: (i, 0, 0))],
        out_specs=[
            pl.BlockSpec((1, nb, LANE), lambda i: (i, 0, 0)),
            pl.BlockSpec((1, 1, cap), lambda i: (i, 0, 0)),
            pl.BlockSpec((1, 1, nb), lambda i: (i, 0, 0)),
        ],
        out_shape=[
            jax.ShapeDtypeStruct((e, nb, LANE), I32),
            jax.ShapeDtypeStruct((e, 1, cap), I32),
            jax.ShapeDtypeStruct((e, 1, nb), I32),
        ],
        compiler_params=_cparams(("arbitrary",)),
        name="route",
    )(aff_t.reshape(e, nb, LANE))


def _expert_kernel(idx_hbm, h_hbm, wg_ref, wu_ref, wd_ref, ye_ref, idx_smem, xbuf, x16, sem_idx, sem_rows,
                   *, rm, nj, nsteps, fc):
    step = pl.program_id(0) * nj + pl.program_id(1)
    slot = step % 2

    def idx_copy(st):
        return pltpu.make_async_copy(idx_hbm.at[st % nsteps], idx_smem.at[st % 3], sem_idx.at[st % 3])

    def row_copy(st, sl, r):
        tok = idx_smem[st % 3, 0, r]
        return pltpu.make_async_copy(h_hbm.at[pl.ds(tok, 1)], xbuf.at[sl, pl.ds(r, 1)], sem_rows.at[sl])

    def rows_wait(sl):
        pltpu.make_async_copy(h_hbm.at[pl.ds(0, rm)], xbuf.at[sl], sem_rows.at[sl]).wait()

    @pl.when(step == 0)
    def _():
        first = idx_copy(0)
        first.start()
        first.wait()
        idx_copy(1).start()

        def issue(r, carry):
            row_copy(0, 0, r).start()
            return carry

        lax.fori_loop(0, rm, issue, 0)

    idx_copy(step + 1).wait()
    idx_copy(step + 2).start()
    rows_wait(slot)

    x16[...] = xbuf[slot].astype(BF16)
    nchunks = D_EXPERT // fc
    per = rm // nchunks
    acc = jnp.zeros((rm, D_MODEL), F32)
    for c in range(nchunks):
        sl = slice(c * fc, (c + 1) * fc)
        g = jnp.dot(x16[...], wg_ref[0, :, sl], preferred_element_type=F32)
        u = jnp.dot(x16[...], wu_ref[0, :, sl], preferred_element_type=F32)
        hid = (_silu(g) * u).astype(BF16)
        acc = acc + jnp.dot(hid, wd_ref[0, sl, :], preferred_element_type=F32)
        for r in range(c * per, (c + 1) * per):
            row_copy(step + 1, 1 - slot, r).start()
    ye_ref[...] = acc.astype(BF16)

    @pl.when(step == nsteps - 1)
    def _():
        idx_copy(step + 2).wait()
        rows_wait(1 - slot)


def _experts(idx, h2, wg16, wu16, wd16, rm):
    e, _, cap = idx.shape
    nj = cap // rm
    t, d = h2.shape
    idx_steps = idx.reshape(e * nj, 1, rm)
    return pl.pallas_call(
        functools.partial(_expert_kernel, rm=rm, nj=nj, nsteps=e * nj, fc=512),
        grid=(e, nj),
        in_specs=[
            pl.BlockSpec(memory_space=pl.ANY),
            pl.BlockSpec(memory_space=pl.ANY),
            pl.BlockSpec((1, d, D_EXPERT), lambda i, j: (i, 0, 0)),
            pl.BlockSpec((1, d, D_EXPERT), lambda i, j: (i, 0, 0)),
            pl.BlockSpec((1, D_EXPERT, d), lambda i, j: (i, 0, 0)),
        ],
        out_specs=pl.BlockSpec((rm, d), lambda i, j: (i * nj + j, 0)),
        out_shape=jax.ShapeDtypeStruct((e * cap, d), BF16),
        scratch_shapes=[
            pltpu.SMEM((3, 1, rm), I32),
            pltpu.VMEM((2, rm, d), F32),
            pltpu.VMEM((rm, d), BF16),
            pltpu.SemaphoreType.DMA((3,)),
            pltpu.SemaphoreType.DMA((2,)),
        ],
        compiler_params=_cparams(("arbitrary", "arbitrary")),
        name="expert",
    )(idx_steps, h2, wg16, wu16, wd16)


def _combine_kernel(starts_ref, q_ref, aff_ref, x1_ref, g2_ref, fnw_ref, ye_hbm, y_ref, ybuf, sem,
                    *, cap, win, short, tm):
    i = pl.program_id(0)
    n = pl.num_programs(0)
    slot = i % 2
    total = N_EXPERTS * cap

    def plan(tile):
        starts, all_short = [], None
        for e in range(N_EXPERTS):
            s0 = starts_ref[tile * N_EXPERTS + e]
            s1 = starts_ref[(tile + 1) * N_EXPERTS + e]
            gs = e * cap + s0
            a = pl.multiple_of(jnp.minimum((gs // BF16_ROWS) * BF16_ROWS, total - win), BF16_ROWS)
            fits = gs + (s1 - s0) - a <= short
            all_short = fits if all_short is None else jnp.logical_and(all_short, fits)
            starts.append(a)
        return starts, all_short

    def for_copies(tile, sl, act):
        starts, all_short = plan(tile)

        def run(rows):
            for e in range(N_EXPERTS):
                act(pltpu.make_async_copy(ye_hbm.at[pl.ds(starts[e], rows)], ybuf.at[sl, e, pl.ds(0, rows)],
                                          sem.at[sl]))

        pl.when(all_short)(lambda: run(short))
        pl.when(jnp.logical_not(all_short))(lambda: run(win))

    @pl.when(i == 0)
    def _():
        ybuf[...] = jnp.zeros_like(ybuf)
        for_copies(0, 0, lambda cp: cp.start())

    @pl.when(i + 1 < n)
    def _():
        for_copies(i + 1, 1 - slot, lambda cp: cp.start())

    for_copies(i, slot, lambda cp: cp.wait())

    q = q_ref[...]
    aff = aff_ref[...]
    w_iota = lax.broadcasted_iota(I32, (tm, win), 1)
    acc = jnp.zeros((tm, D_MODEL), F32)
    win_starts, _ = plan(i)
    for e in range(N_EXPERTS):
        qe = q[:, e:e + 1]
        rel = qe + (e * cap - win_starts[e])
        onehot = jnp.where((qe >= 0) & (rel == w_iota), 1.0, 0.0).astype(BF16)
        part = jnp.dot(onehot, ybuf[slot, e], preferred_element_type=F32)
        acc = acc + aff[:, e:e + 1] * part
    out = x1_ref[...] + g2_ref[0] * acc
    y_ref[...] = _rms(out) * fnw_ref[...]


def _combine(starts_flat, q_tok, aff, x1, g2, final_norm_w, ye, cap, tm, tiles_per_batch):
    t, d = x1.shape
    win = tm + BF16_ROWS
    grid_spec = pltpu.PrefetchScalarGridSpec(
        num_scalar_prefetch=1,
        grid=(t // tm,),
        in_specs=[
            pl.BlockSpec((tm, N_EXPERTS), lambda i, s: (i, 0)),
            pl.BlockSpec((tm, N_EXPERTS), lambda i, s: (i, 0)),
            pl.BlockSpec((tm, d), lambda i, s: (i, 0)),
            pl.BlockSpec((1, 1, d), lambda i, s: (i // tiles_per_batch, 0, 0)),
            pl.BlockSpec((1, d), lambda i, s: (0, 0)),
            pl.BlockSpec(memory_space=pl.ANY),
        ],
        out_specs=pl.BlockSpec((tm, d), lambda i, s: (i, 0)),
        scratch_shapes=[
            pltpu.VMEM((2, N_EXPERTS, win, d), BF16),
            pltpu.SemaphoreType.DMA((2,)),
        ],
    )
    return pl.pallas_call(
        functools.partial(_combine_kernel, cap=cap, win=win, short=3 * BF16_ROWS, tm=tm),
        grid_spec=grid_spec,
        out_shape=jax.ShapeDtypeStruct((t, d), F32),
        compiler_params=_cparams(("arbitrary",)),
        name="combine",
    )(starts_flat, q_tok, aff, x1, g2, final_norm_w.reshape(1, d), ye)


def _rotate_half_cols(w):
    half = ROPE // 2
    return jnp.concatenate([-w[..., half:], w[..., :half]], axis=-1)


def _prep_weights(w_in, w_uq, w_ukv, w_out, w_gate, w_up, w_down):
    d = w_in.shape[0]
    offs = [Q_RANK, Q_RANK + KV_RANK, Q_RANK + KV_RANK + ROPE, Q_RANK + KV_RANK + ROPE + D_SSD,
            Q_RANK + KV_RANK + ROPE + D_SSD + CONV_DIM]
    w_cq, w_ckv, w_kr, w_z, w_xbc, w_dt = jnp.split(w_in, offs, axis=1)
    place = lambda w: jnp.pad(w, ((0, 0), (NOPE, HEAD_PAD - NOPE - ROPE)))
    win_ext = jnp.concatenate(
        [w_cq, w_ckv, place(w_kr), place(_rotate_half_cols(w_kr)), w_z, w_xbc,
         jnp.pad(w_dt, ((0, 0), (0, LANE - w_dt.shape[1])))], axis=1).astype(BF16)
    assert win_ext.shape == (d, _N_IN)

    wq = w_uq.reshape(Q_RANK, N_HEADS, NOPE + ROPE)
    pad_h = lambda w: jnp.pad(w, ((0, 0), (0, 0), (0, HEAD_PAD - NOPE - ROPE))).reshape(Q_RANK, N_HEADS * HEAD_PAD)
    wq_rot = jnp.concatenate([jnp.zeros_like(wq[..., :NOPE]), _rotate_half_cols(wq[..., NOPE:])], axis=-1)
    wq_ext = jnp.concatenate([pad_h(wq), pad_h(wq_rot)], axis=1).astype(BF16)

    wkv = w_ukv.reshape(KV_RANK, N_HEADS, NOPE + V_DIM)
    wk = jnp.pad(wkv[..., :NOPE], ((0, 0), (0, 0), (0, HEAD_PAD - NOPE))).reshape(KV_RANK, N_HEADS * HEAD_PAD)
    wv = jnp.pad(wkv[..., NOPE:], ((0, 0), (0, 0), (0, HEAD_PAD - V_DIM))).reshape(KV_RANK, N_HEADS * HEAD_PAD)
    wkv_ext = jnp.concatenate([wk, wv], axis=1).astype(BF16)
    return win_ext, wq_ext, wkv_ext, w_out.astype(BF16), w_gate.astype(BF16), w_up.astype(BF16), w_down.astype(BF16)


def _rope_table(seq):
    pos = jnp.arange(seq, dtype=F32)
    inv = 1.0 / (ROPE_THETA ** (jnp.arange(0, ROPE, 2, dtype=F32) / ROPE))
    ang = pos[:, None] * inv[None, :]
    cos = jnp.concatenate([jnp.cos(ang)] * 2, axis=-1)
    sin = jnp.concatenate([jnp.sin(ang)] * 2, axis=-1)
    zpad = jnp.zeros((seq, HEAD_PAD - NOPE - ROPE), F32)
    qs = (NOPE + ROPE) ** -0.5 * math.log2(math.e)
    cosq = jnp.concatenate([jnp.ones((seq, NOPE), F32), cos, zpad], axis=-1) * qs
    sinq = jnp.concatenate([jnp.zeros((seq, NOPE), F32), sin, zpad], axis=-1) * qs
    cosk = jnp.concatenate([jnp.zeros((seq, NOPE), F32), cos, zpad], axis=-1)
    sink = jnp.concatenate([jnp.zeros((seq, NOPE), F32), sin, zpad], axis=-1)
    return jnp.concatenate([cosq, sinq, cosk, sink], axis=-1)


def _conv_halo(xbc):
    b, s, c = xbc.shape
    nc = s // CHUNK
    r = xbc.reshape(b, nc, CHUNK, c)
    z2 = jnp.zeros((b, 1, 2, c), xbc.dtype)
    prev = jnp.concatenate([z2, r[:, :-1, CHUNK - 2:, :]], axis=1)
    nxt = jnp.concatenate([r[:, 1:, :2, :], z2], axis=1)
    return jnp.concatenate([prev, nxt, jnp.zeros((b, nc, 4, c), xbc.dtype)], axis=2)


def _trunk(x, mod, p):
    b, s, d = x.shape
    t = b * s
    cap = max(1, CAP_FACTOR * t // N_EXPERTS)
    sh1, sc1, g1, sh2, sc2, g2 = [m.reshape(b, 1, d) for m in jnp.split(mod, 6, axis=-1)]

    tm = min(512, s)
    q, k, v, z, xbc, dt = _inproj(x, sc1, sh1, p["norm1_w"], p["win_ext"], p["q_norm_w"], p["wq_ext"],
                                  p["kv_norm_w"], p["wkv_ext"], _rope_table(s), tm)
    ya = _attention(q, k, v, min(s, max(LANE, _SCORE_ELEMS // s)))
    yf, yb = _ssd(xbc, _conv_halo(xbc), dt, p["conv_w8"], p["conv_b"], p["alog_flat"], p["dtb_flat"], p["dsk_flat"])
    x1, h2, aff_t, aff = _outproj(ya, yf, yb, z, x, g1, sc2, sh2, p["attn_norm_w"], p["ssd_norm_w"], p["wout16"],
                                  p["norm2_w"], p["w_router"], tm)
    q_slot, idx, starts = _route(aff_t, cap)
    ye = _experts(idx, h2.reshape(t, d), p["wg16"], p["wu16"], p["wd16"], min(512, cap))
    q_tok = q_slot.reshape(N_EXPERTS, t).T
    starts_flat = jnp.concatenate([starts.reshape(N_EXPERTS, t // LANE).T,
                                   jnp.full((1, N_EXPERTS), cap, I32)], axis=0).reshape(-1)
    y = _combine(starts_flat, q_tok, aff, x1.reshape(t, d), g2, p["final_norm_w"], ye, cap, LANE, s // LANE)
    return y.reshape(b, s, d)


def kernel(x_prompt, x_sample, c_prompt, c_sample, w_ada, b_ada, norm1_w, w_in, q_norm_w, w_uq, kv_norm_w, w_ukv,
           attn_norm_w, conv_w, conv_b, a_log, dt_bias, d_skip, ssd_norm_w, w_out, norm2_w, w_router, w_gate, w_up,
           w_down, final_norm_w):
    win_ext, wq_ext, wkv_ext, wout16, wg16, wu16, wd16 = _prep_weights(
        w_in[0], w_uq[0], w_ukv[0], w_out[0], w_gate[0], w_up[0], w_down[0])
    flat16 = lambda a: jnp.pad(a.reshape(1, 2 * SSD_HEADS), ((0, 0), (0, LANE - 2 * SSD_HEADS)))
    p = dict(
        norm1_w=norm1_w[0], win_ext=win_ext, q_norm_w=q_norm_w[0], wq_ext=wq_ext, kv_norm_w=kv_norm_w[0],
        wkv_ext=wkv_ext, attn_norm_w=attn_norm_w[0], ssd_norm_w=ssd_norm_w[0], wout16=wout16, norm2_w=norm2_w[0],
        w_router=w_router[0], wg16=wg16, wu16=wu16, wd16=wd16, final_norm_w=final_norm_w,
        conv_w8=jnp.pad(conv_w[0], ((0, 8 - D_CONV), (0, 0))), conv_b=conv_b[0].reshape(1, CONV_DIM),
        alog_flat=flat16(a_log[0]), dtb_flat=flat16(dt_bias[0]),
        dsk_flat=jnp.repeat(d_skip[0], SSD_P).reshape(1, D_SSD),
    )
    nbp = c_prompt.shape[0]
    mod = _mod(jnp.concatenate([c_prompt, c_sample], axis=0), w_ada[0], b_ada[0])
    return _trunk(x_prompt, mod[:nbp], p), _trunk(x_sample, mod[nbp:], p)
```

```python
import functools
import math

import jax
import jax.numpy as jnp
from jax import lax
from jax.experimental import pallas as pl
from jax.experimental.pallas import tpu as pltpu

F32 = jnp.float32
BF16 = jnp.bfloat16
I32 = jnp.int32
HIGHEST = lax.Precision.HIGHEST

D_MODEL = 1024
N_HEADS = 8
NOPE = 64
ROPE = 32
V_DIM = 64
Q_RANK = 256
KV_RANK = 128
ROPE_THETA = 10000.0
D_ATTN = N_HEADS * V_DIM
SSD_HEADS = 8
SSD_P = 64
D_SSD = SSD_HEADS * SSD_P
SSD_G = 2
SSD_N = 64
D_CONV = 5
CHUNK = 128
CONV_DIM = D_SSD + 2 * SSD_G * SSD_N
N_EXPERTS = 16
CAP_FACTOR = 2
D_EXPERT = 2048
EPS = 1e-6

LANE = 128
HEAD_PAD = 128
BF16_ROWS = 16
VMEM_LIMIT = 56 * 1024 * 1024
_SCORE_ELEMS = 1 << 20

_C_CQ = 0
_C_CKV = _C_CQ + Q_RANK
_C_KR = _C_CKV + KV_RANK
_C_KRR = _C_KR + LANE
_C_Z = _C_KRR + LANE
_C_XBC = _C_Z + D_SSD
_C_DT = _C_XBC + CONV_DIM
_N_IN = _C_DT + LANE


def _cparams(sem, vmem=VMEM_LIMIT):
    return pltpu.CompilerParams(dimension_semantics=sem, vmem_limit_bytes=vmem)


def _silu(x):
    return x / (1.0 + jnp.exp(-x))


def _rms(x):
    return x * lax.rsqrt(jnp.mean(x * x, axis=-1, keepdims=True) + EPS)


def _mod_kernel(c_ref, w_ref, b_ref, o_ref):
    a = _silu(c_ref[...])
    o_ref[...] = jnp.dot(a, w_ref[...], precision=HIGHEST, preferred_element_type=F32) + b_ref[...]


def _mod(c, w_ada, b_ada):
    nb, d = c.shape
    n = w_ada.shape[1]
    tn = 1536
    return pl.pallas_call(
        _mod_kernel,
        grid=(n // tn,),
        in_specs=[
            pl.BlockSpec((nb, d), lambda j: (0, 0)),
            pl.BlockSpec((d, tn), lambda j: (0, j)),
            pl.BlockSpec((1, tn), lambda j: (0, j)),
        ],
        out_specs=pl.BlockSpec((nb, tn), lambda j: (0, j)),
        out_shape=jax.ShapeDtypeStruct((nb, n), F32),
        compiler_params=_cparams(("arbitrary",)),
        name="mod",
    )(c, w_ada, b_ada.reshape(1, n))


def _inproj_kernel(x_ref, sc_ref, sh_ref, n1_ref, win_ref, qn_ref, wq_ref, kvn_ref, wkv_ref, tab_ref,
                   q_ref, k_ref, v_ref, z_ref, xbc_ref, dt_ref):
    h = _rms(x_ref[0]) * n1_ref[...]
    h = h * (1.0 + sc_ref[0]) + sh_ref[0]
    proj = jnp.dot(h.astype(BF16), win_ref[...], preferred_element_type=F32)
    z_ref[0] = proj[:, _C_Z:_C_XBC]
    xbc_ref[0] = proj[:, _C_XBC:_C_DT]
    dt_ref[0] = proj[:, _C_DT:_N_IN]

    tab = tab_ref[...]
    cosq, sinq = tab[:, 0:LANE], tab[:, LANE:2 * LANE]
    cosk, sink = tab[:, 2 * LANE:3 * LANE], tab[:, 3 * LANE:4 * LANE]

    cqn = (_rms(proj[:, _C_CQ:_C_CKV]) * qn_ref[...]).astype(BF16)
    q2 = jnp.dot(cqn, wq_ref[...], preferred_element_type=F32)
    ckvn = (_rms(proj[:, _C_CKV:_C_KR]) * kvn_ref[...]).astype(BF16)
    kv = jnp.dot(ckvn, wkv_ref[...], preferred_element_type=F32)
    krope = proj[:, _C_KR:_C_KRR] * cosk + proj[:, _C_KRR:_C_Z] * sink
    nq = N_HEADS * HEAD_PAD
    is_v = lax.broadcasted_iota(I32, (x_ref.shape[1], HEAD_PAD), 1) < V_DIM
    for hd in range(N_HEADS):
        sl = slice(hd * HEAD_PAD, (hd + 1) * HEAD_PAD)
        rot = slice(nq + hd * HEAD_PAD, nq + (hd + 1) * HEAD_PAD)
        q_ref[0, hd] = (q2[:, sl] * cosq + q2[:, rot] * sinq).astype(BF16)
        k_ref[0, hd] = (kv[:, sl] + krope).astype(BF16)
        v_ref[0, hd] = jnp.where(is_v, kv[:, rot], 1.0).astype(BF16)


def _inproj(x, sc1, sh1, norm1_w, win_ext, q_norm_w, wq_ext, kv_norm_w, wkv_ext, tab, tm):
    b, s, d = x.shape
    vec = lambda n: pl.BlockSpec((1, n), lambda i, j: (0, 0))
    full = lambda a: pl.BlockSpec(a.shape, lambda i, j: (0, 0))
    per_b = pl.BlockSpec((1, 1, d), lambda i, j: (i, 0, 0))
    tok = lambda n: pl.BlockSpec((1, tm, n), lambda i, j: (i, j, 0))
    heads = pl.BlockSpec((1, N_HEADS, tm, HEAD_PAD), lambda i, j: (i, 0, j, 0))
    head_shape = jax.ShapeDtypeStruct((b, N_HEADS, s, HEAD_PAD), BF16)
    return pl.pallas_call(
        _inproj_kernel,
        grid=(b, s // tm),
        in_specs=[tok(d), per_b, per_b, vec(d), full(win_ext), vec(Q_RANK), full(wq_ext), vec(KV_RANK),
                  full(wkv_ext), pl.BlockSpec((tm, 4 * LANE), lambda i, j: (j, 0))],
        out_specs=[heads, heads, heads, tok(D_SSD), tok(CONV_DIM), tok(LANE)],
        out_shape=[
            head_shape, head_shape, head_shape,
            jax.ShapeDtypeStruct((b, s, D_SSD), F32),
            jax.ShapeDtypeStruct((b, s, CONV_DIM), F32),
            jax.ShapeDtypeStruct((b, s, LANE), F32),
        ],
        compiler_params=_cparams(("parallel", "arbitrary")),
        name="inproj",
    )(x, sc1, sh1, norm1_w.reshape(1, d), win_ext, q_norm_w.reshape(1, -1), wq_ext,
      kv_norm_w.reshape(1, -1), wkv_ext, tab)


def _attn_kernel(q_ref, k_ref, v_ref, o_ref, s_scr):
    lane = lax.broadcasted_iota(I32, (q_ref.shape[2], HEAD_PAD), 1)
    first = lane < V_DIM

    def scores(h):
        s = lax.dot_general(q_ref[0, h], k_ref[0, h], (((1,), (1,)), ((), ())), preferred_element_type=F32)
        s_scr[h % 2] = s
        return jnp.max(s, axis=-1, keepdims=True)

    def finish(h, m):
        p = jnp.exp2(s_scr[h % 2] - m).astype(BF16)
        o = jnp.dot(p, v_ref[0, h], preferred_element_type=F32)
        return o / pltpu.roll(o, V_DIM, axis=1)

    def emit(h, o, prev):
        if h % 2 == 0:
            return o
        pair = h // 2
        o_ref[0, :, pair * HEAD_PAD:(pair + 1) * HEAD_PAD] = jnp.where(first, prev, pltpu.roll(o, V_DIM, axis=1))
        return None

    m_prev = scores(0)
    held = None
    for h in range(1, N_HEADS):
        m = scores(h)
        held = emit(h - 1, finish(h - 1, m_prev), held)
        m_prev = m
    emit(N_HEADS - 1, finish(N_HEADS - 1, m_prev), held)


def _attention(q, k, v, tq):
    b, _, s, _ = q.shape
    kv_spec = pl.BlockSpec((1, N_HEADS, s, HEAD_PAD), lambda i, j: (i, 0, 0, 0))
    return pl.pallas_call(
        _attn_kernel,
        grid=(b, s // tq),
        in_specs=[pl.BlockSpec((1, N_HEADS, tq, HEAD_PAD), lambda i, j: (i, 0, j, 0)), kv_spec, kv_spec],
        out_specs=pl.BlockSpec((1, tq, D_ATTN), lambda i, j: (i, j, 0)),
        out_shape=jax.ShapeDtypeStruct((b, s, D_ATTN), F32),
        scratch_shapes=[pltpu.VMEM((2, tq, s), F32)],
        compiler_params=_cparams(("parallel", "arbitrary")),
        name="attn",
    )(q, k, v)


def _conv_kernel(x_ref, halo_ref, cw_ref, cb_ref, o_ref, ext_ref):
    n = x_ref.shape[1]
    ext_ref[6:8, :] = halo_ref[0, 0, 0:2, :]
    ext_ref[8:8 + n, :] = x_ref[0]
    ext_ref[8 + n:10 + n, :] = halo_ref[0, 0, 2:4, :]
    conv = cb_ref[...] + cw_ref[0:1, :] * ext_ref[6:6 + n, :]
    for kk in range(1, D_CONV):
        conv = conv + cw_ref[kk:kk + 1, :] * ext_ref[6 + kk:6 + kk + n, :]
    o_ref[0] = _silu(conv)


def _conv(xbc, conv_w8, conv_b, tc):
    b, s, c = xbc.shape
    const = lambda a: pl.BlockSpec(a.shape, lambda i, j: (0, 0))
    return pl.pallas_call(
        _conv_kernel,
        grid=(b, s // tc),
        in_specs=[pl.BlockSpec((1, tc, c), lambda i, j: (i, j, 0)),
                  pl.BlockSpec((1, 1, 8, c), lambda i, j: (i, j, 0, 0)), const(conv_w8), const(conv_b)],
        out_specs=pl.BlockSpec((1, tc, c), lambda i, j: (i, j, 0)),
        out_shape=jax.ShapeDtypeStruct((b, s, c), F32),
        scratch_shapes=[pltpu.VMEM((tc + 16, c), F32)],
        compiler_params=_cparams(("parallel", "arbitrary")),
        name="conv",
    )(xbc, _conv_halo(xbc, tc), conv_w8, conv_b)


def _ssd_direction(act_ref, dt_ref, alog_ref, dtb_ref, dsk_ref, st_ref, y_ref, row, reverse):
    q = CHUNK
    act = act_ref[row]
    bm = act[:, D_SSD:D_SSD + LANE]
    cm = act[:, D_SSD + LANE:D_SSD + 2 * LANE]

    off = SSD_HEADS if reverse else 0
    raw = dt_ref[row] + dtb_ref[...]
    dt = jnp.maximum(raw, 0.0) + jnp.log1p(jnp.exp(-jnp.abs(raw)))
    dta = dt * (-jnp.exp(alog_ref[...]))
    ri = lax.broadcasted_iota(I32, (q, q), 0)
    ci = lax.broadcasted_iota(I32, (q, q), 1)
    mask = (ci >= ri) if reverse else (ci <= ri)
    cs = jnp.dot(mask.astype(F32), dta, precision=HIGHEST, preferred_element_type=F32)
    cs_t = cs.T
    dt_t = dt.T
    end = 0 if reverse else q - 1
    cs_end = cs[end:end + 1, :]
    wdec = jnp.exp(cs_end - cs) * dt
    ecs = jnp.exp(cs)
    cdec = jnp.exp(cs_end)

    lane = lax.broadcasted_iota(I32, (q, LANE), 1)
    sub = lax.broadcasted_iota(I32, (LANE, LANE), 0)
    cm16 = cm.astype(BF16)
    cbs = []
    for g in range(SSD_G):
        in_g = (lane >= g * SSD_N) & (lane < (g + 1) * SSD_N)
        cg = jnp.where(in_g, cm, 0.0).astype(BF16)
        cbs.append(lax.dot_general(cg, bm.astype(BF16), (((1,), (1,)), ((), ())), preferred_element_type=F32))

    for j in range(SSD_HEADS // 2):
        g = (2 * j) // (SSD_HEADS // SSD_G)
        xp = act[:, j * LANE:(j + 1) * LANE]
        xp16 = xp.astype(BF16)
        yd = []
        for hh in (2 * j, 2 * j + 1):
            ln = off + hh
            seg = cs[:, ln:ln + 1] - cs_t[ln:ln + 1, :]
            dec = jnp.exp(jnp.where(mask, seg, -jnp.inf))
            m = (cbs[g] * dec * dt_t[ln:ln + 1, :]).astype(BF16)
            yd.append(jnp.dot(m, xp16, preferred_element_type=F32))
        la, lb = off + 2 * j, off + 2 * j + 1
        first = lane < SSD_P
        y = jnp.where(first, yd[0], yd[1])
        st = st_ref[row, j]
        y_off = lax.dot_general(cm16, st.astype(BF16), (((1,), (1,)), ((), ())), preferred_element_type=F32)
        y = y + y_off * jnp.where(first, ecs[:, la:la + 1], ecs[:, lb:lb + 1])
        xw = (xp * jnp.where(first, wdec[:, la:la + 1], wdec[:, lb:lb + 1])).astype(BF16)
        in_g = (lane >= g * SSD_N) & (lane < (g + 1) * SSD_N)
        bg = jnp.where(in_g, bm, 0.0).astype(BF16)
        new = lax.dot_general(xw, bg, (((0,), (0,)), ((), ())), preferred_element_type=F32)
        st_ref[row, j] = st * jnp.where(sub < SSD_P, cdec[:, la:la + 1], cdec[:, lb:lb + 1]) + new
        if not reverse:
            y = y + dsk_ref[:, j * LANE:(j + 1) * LANE] * xp
        y_ref[row, :, j * LANE:(j + 1) * LANE] = y


def _ssd_kernel(af_ref, dtf_ref, ab_ref, dtb_in_ref, alog_ref, dtbias_ref, dsk_ref, yf_ref, yb_ref, stf_ref, stb_ref):
    @pl.when(pl.program_id(1) == 0)
    def _():
        stf_ref[...] = jnp.zeros_like(stf_ref)
        stb_ref[...] = jnp.zeros_like(stb_ref)

    for row in range(af_ref.shape[0]):
        _ssd_direction(af_ref, dtf_ref, alog_ref, dtbias_ref, dsk_ref, stf_ref, yf_ref, row, reverse=False)
        _ssd_direction(ab_ref, dtb_in_ref, alog_ref, dtbias_ref, dsk_ref, stb_ref, yb_ref, row, reverse=True)


def _ssd(act, dt, alog_flat, dtb_flat, dsk_flat, rows):
    b, s, _ = act.shape
    nc = s // CHUNK
    fwd = lambda n: pl.BlockSpec((rows, CHUNK, n), lambda i, c: (i, c, 0))
    bwd = lambda n: pl.BlockSpec((rows, CHUNK, n), lambda i, c: (i, nc - 1 - c, 0))
    const = lambda a: pl.BlockSpec(a.shape, lambda i, c: (0, 0))
    return pl.pallas_call(
        _ssd_kernel,
        grid=(b // rows, nc),
        in_specs=[fwd(CONV_DIM), fwd(LANE), bwd(CONV_DIM), bwd(LANE),
                  const(alog_flat), const(dtb_flat), const(dsk_flat)],
        out_specs=[fwd(D_SSD), bwd(D_SSD)],
        out_shape=[jax.ShapeDtypeStruct((b, s, D_SSD), F32)] * 2,
        scratch_shapes=[
            pltpu.VMEM((rows, SSD_HEADS // 2, LANE, LANE), F32),
            pltpu.VMEM((rows, SSD_HEADS // 2, LANE, LANE), F32),
        ],
        compiler_params=_cparams(("parallel", "arbitrary")),
        name="ssd",
    )(act, dt, act, dt, alog_flat, dtb_flat, dsk_flat)


def _outproj_kernel(ya_ref, yf_ref, yb_ref, z_ref, x_ref, g1_ref, sc2_ref, sh2_ref, anw_ref, snw_ref, wout_ref,
                    n2_ref, wrt_ref, x1_ref, h2_ref, afft_ref, aff_ref):
    ya = _rms(ya_ref[0]) * anw_ref[...]
    y = (yf_ref[0] + yb_ref[0]) * _silu(z_ref[0])
    half = D_SSD // SSD_G
    snw = snw_ref[...]
    parts = [ya]
    for g in range(SSD_G):
        sl = slice(g * half, (g + 1) * half)
        parts.append(_rms(y[:, sl]) * snw[:, sl])
    ycat = jnp.concatenate(parts, axis=-1).astype(BF16)
    o = jnp.dot(ycat, wout_ref[...], preferred_element_type=F32)
    x1 = x_ref[0] + g1_ref[0] * o
    x1_ref[0] = x1
    h2 = _rms(x1) * n2_ref[...]
    h2 = h2 * (1.0 + sc2_ref[0]) + sh2_ref[0]
    h2_ref[0] = h2

    lt = lax.dot_general(wrt_ref[...], h2, (((1,), (1,)), ((), ())), precision=HIGHEST,
                         preferred_element_type=F32)
    et = jnp.exp(lt - jnp.max(lt, axis=0, keepdims=True))
    aff_t = et / jnp.sum(et, axis=0, keepdims=True)
    afft_ref[...] = aff_t
    padded = jnp.concatenate([aff_t, jnp.zeros((LANE - N_EXPERTS, aff_t.shape[1]), F32)], axis=0)
    aff_ref[...] = padded.T[:, :N_EXPERTS]


def _outproj(ya, yf, yb, z, x, g1, sc2, sh2, attn_norm_w, ssd_norm_w, wout16, norm2_w, w_router, tm):
    b, s, d = x.shape
    t = b * s
    nt = s // tm
    tok = lambda n: pl.BlockSpec((1, tm, n), lambda i, j: (i, j, 0))
    per_b = pl.BlockSpec((1, 1, d), lambda i, j: (i, 0, 0))
    vec = lambda n: pl.BlockSpec((1, n), lambda i, j: (0, 0))
    full = lambda a: pl.BlockSpec(a.shape, lambda i, j: (0, 0))
    wrt = w_router.T
    return pl.pallas_call(
        _outproj_kernel,
        grid=(b, nt),
        in_specs=[tok(D_ATTN), tok(D_SSD), tok(D_SSD), tok(D_SSD), tok(d), per_b, per_b, per_b,
                  vec(D_ATTN), vec(D_SSD), full(wout16), vec(d), full(wrt)],
        out_specs=[tok(d), tok(d),
                   pl.BlockSpec((N_EXPERTS, tm), lambda i, j: (0, i * nt + j)),
                   pl.BlockSpec((tm, N_EXPERTS), lambda i, j: (i * nt + j, 0))],
        out_shape=[
            jax.ShapeDtypeStruct((b, s, d), F32),
            jax.ShapeDtypeStruct((b, s, d), F32),
            jax.ShapeDtypeStruct((N_EXPERTS, t), F32),
            jax.ShapeDtypeStruct((t, N_EXPERTS), F32),
        ],
        compiler_params=_cparams(("parallel", "arbitrary")),
        name="outproj",
    )(ya, yf, yb, z, x, g1, sc2, sh2, attn_norm_w.reshape(1, -1), ssd_norm_w.reshape(1, -1), wout16,
      norm2_w.reshape(1, d), wrt)


def _count(cond):
    c = jnp.sum(jnp.where(cond, 1.0, 0.0), axis=0, keepdims=True)
    return jnp.sum(c, axis=1, keepdims=True)


def _route_kernel(aff_ref, q_ref, idx_ref, starts_ref, *, cap, jc):
    a = aff_ref[0]
    nb = a.shape[0]
    keys = pltpu.bitcast(a, I32)
    kf = jnp.float32(cap)

    tau = jnp.zeros((1, 1), I32)
    for bit in range(30, -1, -1):
        cand = tau | jnp.int32(1 << bit)
        tau = jnp.where(_count(keys >= cand) >= kf, cand, tau)
    gt = keys > tau
    eq = keys == tau
    need = kf - _count(gt)

    r128 = lax.broadcasted_iota(I32, (LANE, LANE), 0)
    c128 = lax.broadcasted_iota(I32, (LANE, LANE), 1)
    upper = (r128 <= c128).astype(BF16)
    ones = jnp.ones((LANE, LANE), BF16)
    rb = lax.broadcasted_iota(I32, (nb, nb), 0)
    cb = lax.broadcasted_iota(I32, (nb, nb), 1)
    lower_strict = (cb < rb).astype(BF16)

    def prefix(m16):
        incl = jnp.dot(m16, upper, preferred_element_type=F32)
        cnt = jnp.dot(m16, ones, preferred_element_type=F32)
        bex = jnp.dot(lower_strict, cnt.astype(BF16), preferred_element_type=F32)
        return bex + incl - m16.astype(F32), cnt, bex

    gt16 = jnp.where(gt, 1.0, 0.0).astype(BF16)
    eq16 = jnp.where(eq, 1.0, 0.0).astype(BF16)
    gt_ex, _, _ = prefix(gt16)
    eq_ex, _, _ = prefix(eq16)
    sel = gt | (eq & (eq_ex < need))
    pos = gt_ex + jnp.minimum(eq_ex, need)
    q_ref[0] = jnp.where(sel, pos, -1.0).astype(I32)

    sel16 = jnp.where(sel, 1.0, 0.0).astype(BF16)
    _, cnt_c, bex_c = prefix(sel16)
    bin_c = bex_c + cnt_c
    lower_incl = (c128 <= r128).astype(BF16)
    incl_t = lax.dot_general(lower_incl, sel16, (((1,), (1,)), ((), ())), preferred_element_type=F32)
    cnt_r = lax.dot_general(jnp.ones((8, LANE), BF16), sel16, (((1,), (1,)), ((), ())),
                            preferred_element_type=F32)
    upper_strict = (rb < cb).astype(BF16)
    bex_r = jnp.dot(cnt_r.astype(BF16), upper_strict, preferred_element_type=F32)
    starts_ref[0] = bex_r[0:1, :].astype(I32)
    g_t = incl_t + bex_r[0:1, :]
    g_hi = jnp.floor(g_t * (1.0 / 64.0))
    g_lo = g_t - 64.0 * g_hi
    g_hi16 = g_hi.astype(BF16)
    g_lo16 = g_lo.astype(BF16)
    reps = jc // LANE
    bex_w = jnp.concatenate([bex_c] * reps, axis=1)
    bin_w = jnp.concatenate([bin_c] * reps, axis=1)
    lane_nb = lax.broadcasted_iota(I32, (nb, jc), 1).astype(F32)
    lane_128 = lax.broadcasted_iota(I32, (LANE, jc), 1).astype(F32)

    def chunk(c, carry):
        base = jnp.asarray(c * jc, I32).astype(F32)
        j_nb = lane_nb + base
        onehot_t = jnp.where((bex_w <= j_nb) & (j_nb < bin_w), 1.0, 0.0).astype(BF16)
        rows = (64.0 * jnp.dot(g_hi16, onehot_t, preferred_element_type=F32)
                + jnp.dot(g_lo16, onehot_t, preferred_element_type=F32))
        offs = jnp.sum(jnp.where(rows <= lane_128 + base, 1.0, 0.0), axis=0, keepdims=True)
        blk = jnp.sum(jnp.where(bin_w <= j_nb, 1.0, 0.0), axis=0, keepdims=True)
        idx_ref[0, :, pl.ds(pl.multiple_of(c * jc, jc), jc)] = (blk * float(LANE) + offs).astype(I32)
        return carry

    lax.fori_loop(0, cap // jc, chunk, 0)


def _route(aff_t, cap):
    e, t = aff_t.shape
    nb = t // LANE
    jc = min(256, cap)
    return pl.pallas_call(
        functools.partial(_route_kernel, cap=cap, jc=jc),
        grid=(e,),
        in_specs=[pl.BlockSpec((1, nb, LANE), lambda i: (i, 0, 0))],
        out_specs=[
            pl.BlockSpec((1, nb, LANE), lambda i: (i, 0, 0)),
            pl.BlockSpec((1, 1, cap), lambda i: (i, 0, 0)),
            pl.BlockSpec((1, 1, nb), lambda i: (i, 0, 0)),
        ],
        out_shape=[
            jax.ShapeDtypeStruct((e, nb, LANE), I32),
            jax.ShapeDtypeStruct((e, 1, cap), I32),
            jax.ShapeDtypeStruct((e, 1, nb), I32),
        ],
        compiler_params=_cparams(("arbitrary",)),
        name="route",
    )(aff_t.reshape(e, nb, LANE))


def _expert_kernel(idx_hbm, h_hbm, wg_ref, wu_ref, wd_ref, ye_ref, idx_smem, xbuf, x16, sem_idx, sem_rows,
                   *, rm, nj, nsteps, fc):
    step = pl.program_id(0) * nj + pl.program_id(1)
    slot = step % 2

    def idx_copy(st):
        return pltpu.make_async_copy(idx_hbm.at[st % nsteps], idx_smem.at[st % 3], sem_idx.at[st % 3])

    def row_copy(st, sl, r):
        tok = idx_smem[st % 3, 0, r]
        return pltpu.make_async_copy(h_hbm.at[pl.ds(tok, 1)], xbuf.at[sl, pl.ds(r, 1)], sem_rows.at[sl])

    def rows_wait(sl):
        pltpu.make_async_copy(h_hbm.at[pl.ds(0, rm)], xbuf.at[sl], sem_rows.at[sl]).wait()

    @pl.when(step == 0)
    def _():
        first = idx_copy(0)
        first.start()
        first.wait()
        idx_copy(1).start()

        def issue(r, carry):
            row_copy(0, 0, r).start()
            return carry

        lax.fori_loop(0, rm, issue, 0)

    idx_copy(step + 1).wait()
    idx_copy(step + 2).start()
    for r in range(rm):
        row_copy(step + 1, 1 - slot, r).start()
    rows_wait(slot)

    x16[...] = xbuf[slot].astype(BF16)
    acc = jnp.zeros((rm, D_MODEL), F32)
    for c in range(D_EXPERT // fc):
        sl = slice(c * fc, (c + 1) * fc)
        g = jnp.dot(x16[...], wg_ref[0, :, sl], preferred_element_type=F32)
        u = jnp.dot(x16[...], wu_ref[0, :, sl], preferred_element_type=F32)
        hid = (_silu(g) * u).astype(BF16)
        acc = acc + jnp.dot(hid, wd_ref[0, sl, :], preferred_element_type=F32)
    ye_ref[...] = acc.astype(BF16)

    @pl.when(step == nsteps - 1)
    def _():
        idx_copy(step + 2).wait()
        rows_wait(1 - slot)


def _experts(idx, h2, wg16, wu16, wd16, rm):
    e, _, cap = idx.shape
    nj = cap // rm
    t, d = h2.shape
    idx_steps = idx.reshape(e * nj, 1, rm)
    return pl.pallas_call(
        functools.partial(_expert_kernel, rm=rm, nj=nj, nsteps=e * nj, fc=512),
        grid=(e, nj),
        in_specs=[
            pl.BlockSpec(memory_space=pl.ANY),
            pl.BlockSpec(memory_space=pl.ANY),
            pl.BlockSpec((1, d, D_EXPERT), lambda i, j: (i, 0, 0)),
            pl.BlockSpec((1, d, D_EXPERT), lambda i, j: (i, 0, 0)),
            pl.BlockSpec((1, D_EXPERT, d), lambda i, j: (i, 0, 0)),
        ],
        out_specs=pl.BlockSpec((rm, d), lambda i, j: (i * nj + j, 0)),
        out_shape=jax.ShapeDtypeStruct((e * cap, d), BF16),
        scratch_shapes=[
            pltpu.SMEM((3, 1, rm), I32),
            pltpu.VMEM((2, rm, d), F32),
            pltpu.VMEM((rm, d), BF16),
            pltpu.SemaphoreType.DMA((3,)),
            pltpu.SemaphoreType.DMA((2,)),
        ],
        compiler_params=_cparams(("arbitrary", "arbitrary")),
        name="expert",
    )(idx_steps, h2, wg16, wu16, wd16)


def _combine_kernel(starts_ref, q_ref, aff_ref, x1_ref, g2_ref, fnw_ref, ye_hbm, y_ref, ybuf, ybig, acc_ref,
                    sem, sem_big, *, cap, win, short, tm):
    i = pl.program_id(0)
    n = pl.num_programs(0)
    slot = i % 2
    total = N_EXPERTS * cap
    per_col = LANE // short

    def plan(tile):
        starts, all_short = [], None
        for e in range(N_EXPERTS):
            s0 = starts_ref[tile * N_EXPERTS + e]
            s1 = starts_ref[(tile + 1) * N_EXPERTS + e]
            gs = e * cap + s0
            a = pl.multiple_of(jnp.minimum((gs // BF16_ROWS) * BF16_ROWS, total - win), BF16_ROWS)
            fits = gs + (s1 - s0) - a <= short
            all_short = fits if all_short is None else jnp.logical_and(all_short, fits)
            starts.append(a)
        return starts, all_short

    def short_copies(tile, sl):
        starts, _ = plan(tile)
        return [pltpu.make_async_copy(ye_hbm.at[pl.ds(starts[e], short)], ybuf.at[sl, pl.ds(e * short, short)],
                                      sem.at[sl]) for e in range(N_EXPERTS)]

    @pl.when(i == 0)
    def _():
        for cp in short_copies(0, 0):
            cp.start()

    @pl.when(i + 1 < n)
    def _():
        for cp in short_copies(i + 1, 1 - slot):
            cp.start()

    for cp in short_copies(i, slot):
        cp.wait()

    starts, all_short = plan(i)
    q = q_ref[...]
    aff = aff_ref[...]

    @pl.when(all_short)
    def _():
        lane = lax.broadcasted_iota(I32, (tm, LANE), 1)
        cols = []
        for c in range(N_EXPERTS // per_col):
            val = jnp.zeros((tm, LANE), F32)
            for k in range(per_col):
                e = c * per_col + k
                qe = q[:, e:e + 1]
                rel = qe + (e * cap - starts[e] + k * short)
                val = jnp.where((qe >= 0) & (rel == lane), aff[:, e:e + 1], val)
            cols.append(val)
        val = jnp.concatenate(cols, axis=1)
        hi = val.astype(BF16)
        lo = (val - hi.astype(F32)).astype(BF16)
        both = jnp.dot(jnp.concatenate([hi, lo], axis=0), ybuf[slot], preferred_element_type=F32)
        acc_ref[...] = both[:tm] + both[tm:]

    @pl.when(jnp.logical_not(all_short))
    def _():
        wide = [pltpu.make_async_copy(ye_hbm.at[pl.ds(starts[e], win)], ybig.at[e], sem_big.at[0])
                for e in range(N_EXPERTS)]
        for cp in wide:
            cp.start()
        for cp in wide:
            cp.wait()
        w_iota = lax.broadcasted_iota(I32, (tm, win), 1)
        acc = jnp.zeros((tm, D_MODEL), F32)
        for e in range(N_EXPERTS):
            qe = q[:, e:e + 1]
            rel = qe + (e * cap - starts[e])
            onehot = jnp.where((qe >= 0) & (rel == w_iota), 1.0, 0.0).astype(BF16)
            acc = acc + aff[:, e:e + 1] * jnp.dot(onehot, ybig[e], preferred_element_type=F32)
        acc_ref[...] = acc

    out = x1_ref[...] + g2_ref[0] * acc_ref[...]
    y_ref[...] = _rms(out) * fnw_ref[...]


def _combine(starts_flat, q_tok, aff, x1, g2, final_norm_w, ye, cap, tm, tiles_per_batch, short=LANE // 2):
    t, d = x1.shape
    win = tm + BF16_ROWS
    grid_spec = pltpu.PrefetchScalarGridSpec(
        num_scalar_prefetch=1,
        grid=(t // tm,),
        in_specs=[
            pl.BlockSpec((tm, N_EXPERTS), lambda i, s: (i, 0)),
            pl.BlockSpec((tm, N_EXPERTS), lambda i, s: (i, 0)),
            pl.BlockSpec((tm, d), lambda i, s: (i, 0)),
            pl.BlockSpec((1, 1, d), lambda i, s: (i // tiles_per_batch, 0, 0)),
            pl.BlockSpec((1, d), lambda i, s: (0, 0)),
            pl.BlockSpec(memory_space=pl.ANY),
        ],
        out_specs=pl.BlockSpec((tm, d), lambda i, s: (i, 0)),
        scratch_shapes=[
            pltpu.VMEM((2, N_EXPERTS * short, d), BF16),
            pltpu.VMEM((N_EXPERTS, win, d), BF16),
            pltpu.VMEM((tm, d), F32),
            pltpu.SemaphoreType.DMA((2,)),
            pltpu.SemaphoreType.DMA((1,)),
        ],
    )
    return pl.pallas_call(
        functools.partial(_combine_kernel, cap=cap, win=win, short=short, tm=tm),
        grid_spec=grid_spec,
        out_shape=jax.ShapeDtypeStruct((t, d), F32),
        compiler_params=_cparams(("arbitrary",)),
        name="combine",
    )(starts_flat, q_tok, aff, x1, g2, final_norm_w.reshape(1, d), ye)


def _rotate_half_cols(w):
    half = ROPE // 2
    return jnp.concatenate([-w[..., half:], w[..., :half]], axis=-1)


def _prep_weights(w_in, w_uq, w_ukv, w_out, w_gate, w_up, w_down):
    d = w_in.shape[0]
    offs = [Q_RANK, Q_RANK + KV_RANK, Q_RANK + KV_RANK + ROPE, Q_RANK + KV_RANK + ROPE + D_SSD,
            Q_RANK + KV_RANK + ROPE + D_SSD + CONV_DIM]
    w_cq, w_ckv, w_kr, w_z, w_xbc, w_dt = jnp.split(w_in, offs, axis=1)
    place = lambda w: jnp.pad(w, ((0, 0), (NOPE, HEAD_PAD - NOPE - ROPE)))
    win_ext = jnp.concatenate(
        [w_cq, w_ckv, place(w_kr), place(_rotate_half_cols(w_kr)), w_z, w_xbc,
         jnp.pad(w_dt, ((0, 0), (0, LANE - w_dt.shape[1])))], axis=1).astype(BF16)
    assert win_ext.shape == (d, _N_IN)

    wq = w_uq.reshape(Q_RANK, N_HEADS, NOPE + ROPE)
    pad_h = lambda w: jnp.pad(w, ((0, 0), (0, 0), (0, HEAD_PAD - NOPE - ROPE))).reshape(Q_RANK, N_HEADS * HEAD_PAD)
    wq_rot = jnp.concatenate([jnp.zeros_like(wq[..., :NOPE]), _rotate_half_cols(wq[..., NOPE:])], axis=-1)
    wq_ext = jnp.concatenate([pad_h(wq), pad_h(wq_rot)], axis=1).astype(BF16)

    wkv = w_ukv.reshape(KV_RANK, N_HEADS, NOPE + V_DIM)
    wk = jnp.pad(wkv[..., :NOPE], ((0, 0), (0, 0), (0, HEAD_PAD - NOPE))).reshape(KV_RANK, N_HEADS * HEAD_PAD)
    wv = jnp.pad(wkv[..., NOPE:], ((0, 0), (0, 0), (0, HEAD_PAD - V_DIM))).reshape(KV_RANK, N_HEADS * HEAD_PAD)
    wkv_ext = jnp.concatenate([wk, wv], axis=1).astype(BF16)
    return win_ext, wq_ext, wkv_ext, w_out.astype(BF16), w_gate.astype(BF16), w_up.astype(BF16), w_down.astype(BF16)


def _rope_table(seq):
    pos = jnp.arange(seq, dtype=F32)
    inv = 1.0 / (ROPE_THETA ** (jnp.arange(0, ROPE, 2, dtype=F32) / ROPE))
    ang = pos[:, None] * inv[None, :]
    cos = jnp.concatenate([jnp.cos(ang)] * 2, axis=-1)
    sin = jnp.concatenate([jnp.sin(ang)] * 2, axis=-1)
    zpad = jnp.zeros((seq, HEAD_PAD - NOPE - ROPE), F32)
    qs = (NOPE + ROPE) ** -0.5 * math.log2(math.e)
    cosq = jnp.concatenate([jnp.ones((seq, NOPE), F32), cos, zpad], axis=-1) * qs
    sinq = jnp.concatenate([jnp.zeros((seq, NOPE), F32), sin, zpad], axis=-1) * qs
    cosk = jnp.concatenate([jnp.zeros((seq, NOPE), F32), cos, zpad], axis=-1)
    sink = jnp.concatenate([jnp.zeros((seq, NOPE), F32), sin, zpad], axis=-1)
    return jnp.concatenate([cosq, sinq, cosk, sink], axis=-1)


def _conv_halo(xbc, tile):
    b, s, c = xbc.shape
    nc = s // tile
    r = xbc.reshape(b, nc, tile, c)
    z2 = jnp.zeros((b, 1, 2, c), xbc.dtype)
    prev = jnp.concatenate([z2, r[:, :-1, tile - 2:, :]], axis=1)
    nxt = jnp.concatenate([r[:, 1:, :2, :], z2], axis=1)
    return jnp.concatenate([prev, nxt, jnp.zeros((b, nc, 4, c), xbc.dtype)], axis=2)


def _trunk(x, mod, p):
    b, s, d = x.shape
    t = b * s
    cap = max(1, CAP_FACTOR * t // N_EXPERTS)
    sh1, sc1, g1, sh2, sc2, g2 = [m.reshape(b, 1, d) for m in jnp.split(mod, 6, axis=-1)]

    tm = min(512, s)
    q, k, v, z, xbc, dt = _inproj(x, sc1, sh1, p["norm1_w"], p["win_ext"], p["q_norm_w"], p["wq_ext"],
                                  p["kv_norm_w"], p["wkv_ext"], _rope_table(s), tm)
    ya = _attention(q, k, v, min(s, max(LANE, _SCORE_ELEMS // s)))
    act = _conv(xbc, p["conv_w8"], p["conv_b"], tm)
    yf, yb = _ssd(act, dt, p["alog_flat"], p["dtb_flat"], p["dsk_flat"], 2 if b % 2 == 0 else 1)
    x1, h2, aff_t, aff = _outproj(ya, yf, yb, z, x, g1, sc2, sh2, p["attn_norm_w"], p["ssd_norm_w"], p["wout16"],
                                  p["norm2_w"], p["w_router"], tm)
    q_slot, idx, starts = _route(aff_t, cap)
    ye = _experts(idx, h2.reshape(t, d), p["wg16"], p["wu16"], p["wd16"], min(512, cap))
    q_tok = q_slot.reshape(N_EXPERTS, t).T
    starts_flat = jnp.concatenate([starts.reshape(N_EXPERTS, t // LANE).T,
                                   jnp.full((1, N_EXPERTS), cap, I32)], axis=0).reshape(-1)
    y = _combine(starts_flat, q_tok, aff, x1.reshape(t, d), g2, p["final_norm_w"], ye, cap, LANE, s // LANE)
    return y.reshape(b, s, d)


def kernel(x_prompt, x_sample, c_prompt, c_sample, w_ada, b_ada, norm1_w, w_in, q_norm_w, w_uq, kv_norm_w, w_ukv,
           attn_norm_w, conv_w, conv_b, a_log, dt_bias, d_skip, ssd_norm_w, w_out, norm2_w, w_router, w_gate, w_up,
           w_down, final_norm_w):
    win_ext, wq_ext, wkv_ext, wout16, wg16, wu16, wd16 = _prep_weights(
        w_in[0], w_uq[0], w_ukv[0], w_out[0], w_gate[0], w_up[0], w_down[0])
    flat16 = lambda a: jnp.pad(a.reshape(1, 2 * SSD_HEADS), ((0, 0), (0, LANE - 2 * SSD_HEADS)))
    p = dict(
        norm1_w=norm1_w[0], win_ext=win_ext, q_norm_w=q_norm_w[0], wq_ext=wq_ext, kv_norm_w=kv_norm_w[0],
        wkv_ext=wkv_ext, attn_norm_w=attn_norm_w[0], ssd_norm_w=ssd_norm_w[0], wout16=wout16, norm2_w=norm2_w[0],
        w_router=w_router[0], wg16=wg16, wu16=wu16, wd16=wd16, final_norm_w=final_norm_w,
        conv_w8=jnp.pad(conv_w[0], ((0, 8 - D_CONV), (0, 0))), conv_b=conv_b[0].reshape(1, CONV_DIM),
        alog_flat=flat16(a_log[0]), dtb_flat=flat16(dt_bias[0]),
        dsk_flat=jnp.repeat(d_skip[0], SSD_P).reshape(1, D_SSD),
    )
    nbp = c_prompt.shape[0]
    mod = _mod(jnp.concatenate([c_prompt, c_sample], axis=0), w_ada[0], b_ada[0])
    return _trunk(x_prompt, mod[:nbp], p), _trunk(x_sample, mod[nbp:], p)
```

```python
import functools
import math

import jax
import jax.numpy as jnp
from jax import lax
from jax.experimental import pallas as pl
from jax.experimental.pallas import tpu as pltpu

F32 = jnp.float32
BF16 = jnp.bfloat16
I32 = jnp.int32
HIGHEST = lax.Precision.HIGHEST

D_MODEL = 1024
N_HEADS = 8
NOPE = 64
ROPE = 32
V_DIM = 64
Q_RANK = 256
KV_RANK = 128
ROPE_THETA = 10000.0
D_ATTN = N_HEADS * V_DIM
SSD_HEADS = 8
SSD_P = 64
D_SSD = SSD_HEADS * SSD_P
SSD_G = 2
SSD_N = 64
D_CONV = 5
CHUNK = 128
CONV_DIM = D_SSD + 2 * SSD_G * SSD_N
N_EXPERTS = 16
CAP_FACTOR = 2
D_EXPERT = 2048
EPS = 1e-6

LANE = 128
HEAD_PAD = 128
BF16_ROWS = 16
VMEM_LIMIT = 56 * 1024 * 1024
_SCORE_ELEMS = 1 << 20

_C_CQ = 0
_C_CKV = _C_CQ + Q_RANK
_C_KR = _C_CKV + KV_RANK
_C_KRR = _C_KR + LANE
_C_Z = _C_KRR + LANE
_C_XBC = _C_Z + D_SSD
_C_DT = _C_XBC + CONV_DIM
_N_IN = _C_DT + LANE


def _cparams(sem, vmem=VMEM_LIMIT):
    return pltpu.CompilerParams(dimension_semantics=sem, vmem_limit_bytes=vmem)


def _silu(x):
    return x / (1.0 + jnp.exp(-x))


def _rms(x):
    return x * lax.rsqrt(jnp.mean(x * x, axis=-1, keepdims=True) + EPS)


def _mod_kernel(c_ref, w_ref, b_ref, o_ref):
    a = _silu(c_ref[...])
    o_ref[...] = jnp.dot(a, w_ref[...], precision=HIGHEST, preferred_element_type=F32) + b_ref[...]


def _mod(c, w_ada, b_ada):
    nb, d = c.shape
    n = w_ada.shape[1]
    tn = 1536
    return pl.pallas_call(
        _mod_kernel,
        grid=(n // tn,),
        in_specs=[
            pl.BlockSpec((nb, d), lambda j: (0, 0)),
            pl.BlockSpec((d, tn), lambda j: (0, j)),
            pl.BlockSpec((1, tn), lambda j: (0, j)),
        ],
        out_specs=pl.BlockSpec((nb, tn), lambda j: (0, j)),
        out_shape=jax.ShapeDtypeStruct((nb, n), F32),
        compiler_params=_cparams(("arbitrary",)),
        name="mod",
    )(c, w_ada, b_ada.reshape(1, n))


def _inproj_kernel(x_ref, sc_ref, sh_ref, n1_ref, win_ref, qn_ref, wq_ref, kvn_ref, wkv_ref, tab_ref,
                   q_ref, k_ref, v_ref, z_ref, xbc_ref, dt_ref):
    h = _rms(x_ref[0]) * n1_ref[...]
    h = h * (1.0 + sc_ref[0]) + sh_ref[0]
    proj = jnp.dot(h.astype(BF16), win_ref[...], preferred_element_type=F32)
    z_ref[0] = proj[:, _C_Z:_C_XBC]
    xbc_ref[0] = proj[:, _C_XBC:_C_DT]
    dt_ref[0] = proj[:, _C_DT:_N_IN]

    tab = tab_ref[...]
    cosq, sinq = tab[:, 0:LANE], tab[:, LANE:2 * LANE]
    cosk, sink = tab[:, 2 * LANE:3 * LANE], tab[:, 3 * LANE:4 * LANE]

    cqn = (_rms(proj[:, _C_CQ:_C_CKV]) * qn_ref[...]).astype(BF16)
    q2 = jnp.dot(cqn, wq_ref[...], preferred_element_type=F32)
    ckvn = (_rms(proj[:, _C_CKV:_C_KR]) * kvn_ref[...]).astype(BF16)
    kv = jnp.dot(ckvn, wkv_ref[...], preferred_element_type=F32)
    krope = proj[:, _C_KR:_C_KRR] * cosk + proj[:, _C_KRR:_C_Z] * sink
    nq = N_HEADS * HEAD_PAD
    is_v = lax.broadcasted_iota(I32, (x_ref.shape[1], HEAD_PAD), 1) < V_DIM
    for hd in range(N_HEADS):
        sl = slice(hd * HEAD_PAD, (hd + 1) * HEAD_PAD)
        rot = slice(nq + hd * HEAD_PAD, nq + (hd + 1) * HEAD_PAD)
        q_ref[0, hd] = (q2[:, sl] * cosq + q2[:, rot] * sinq).astype(BF16)
        k_ref[0, hd] = (kv[:, sl] + krope).astype(BF16)
        v_ref[0, hd] = jnp.where(is_v, kv[:, rot], 1.0).astype(BF16)


def _inproj(x, sc1, sh1, norm1_w, win_ext, q_norm_w, wq_ext, kv_norm_w, wkv_ext, tab, tm):
    b, s, d = x.shape
    vec = lambda n: pl.BlockSpec((1, n), lambda i, j: (0, 0))
    full = lambda a: pl.BlockSpec(a.shape, lambda i, j: (0, 0))
    per_b = pl.BlockSpec((1, 1, d), lambda i, j: (i, 0, 0))
    tok = lambda n: pl.BlockSpec((1, tm, n), lambda i, j: (i, j, 0))
    heads = pl.BlockSpec((1, N_HEADS, tm, HEAD_PAD), lambda i, j: (i, 0, j, 0))
    head_shape = jax.ShapeDtypeStruct((b, N_HEADS, s, HEAD_PAD), BF16)
    return pl.pallas_call(
        _inproj_kernel,
        grid=(b, s // tm),
        in_specs=[tok(d), per_b, per_b, vec(d), full(win_ext), vec(Q_RANK), full(wq_ext), vec(KV_RANK),
                  full(wkv_ext), pl.BlockSpec((tm, 4 * LANE), lambda i, j: (j, 0))],
        out_specs=[heads, heads, heads, tok(D_SSD), tok(CONV_DIM), tok(LANE)],
        out_shape=[
            head_shape, head_shape, head_shape,
            jax.ShapeDtypeStruct((b, s, D_SSD), F32),
            jax.ShapeDtypeStruct((b, s, CONV_DIM), F32),
            jax.ShapeDtypeStruct((b, s, LANE), F32),
        ],
        compiler_params=_cparams(("parallel", "arbitrary")),
        name="inproj",
    )(x, sc1, sh1, norm1_w.reshape(1, d), win_ext, q_norm_w.reshape(1, -1), wq_ext,
      kv_norm_w.reshape(1, -1), wkv_ext, tab)


def _attn_kernel(q_ref, k_ref, v_ref, o_ref, s_scr):
    lane = lax.broadcasted_iota(I32, (q_ref.shape[2], HEAD_PAD), 1)
    first = lane < V_DIM

    def scores(h):
        s = lax.dot_general(q_ref[0, h], k_ref[0, h], (((1,), (1,)), ((), ())), preferred_element_type=F32)
        s_scr[h % 2] = s
        return jnp.max(s, axis=-1, keepdims=True)

    def finish(h, m):
        p = jnp.exp2(s_scr[h % 2] - m).astype(BF16)
        o = jnp.dot(p, v_ref[0, h], preferred_element_type=F32)
        return o / pltpu.roll(o, V_DIM, axis=1)

    def emit(h, o, prev):
        if h % 2 == 0:
            return o
        pair = h // 2
        o_ref[0, :, pair * HEAD_PAD:(pair + 1) * HEAD_PAD] = jnp.where(first, prev, pltpu.roll(o, V_DIM, axis=1))
        return None

    m_prev = scores(0)
    held = None
    for h in range(1, N_HEADS):
        m = scores(h)
        held = emit(h - 1, finish(h - 1, m_prev), held)
        m_prev = m
    emit(N_HEADS - 1, finish(N_HEADS - 1, m_prev), held)


def _attention(q, k, v, tq):
    b, _, s, _ = q.shape
    kv_spec = pl.BlockSpec((1, N_HEADS, s, HEAD_PAD), lambda i, j: (i, 0, 0, 0))
    return pl.pallas_call(
        _attn_kernel,
        grid=(b, s // tq),
        in_specs=[pl.BlockSpec((1, N_HEADS, tq, HEAD_PAD), lambda i, j: (i, 0, j, 0)), kv_spec, kv_spec],
        out_specs=pl.BlockSpec((1, tq, D_ATTN), lambda i, j: (i, j, 0)),
        out_shape=jax.ShapeDtypeStruct((b, s, D_ATTN), F32),
        scratch_shapes=[pltpu.VMEM((2, tq, s), F32)],
        compiler_params=_cparams(("parallel", "arbitrary")),
        name="attn",
    )(q, k, v)


def _ssd_direction(xbc_ref, halo_ref, dt_ref, cw_ref, cb_ref, alog_ref, dtb_ref, dsk_ref, ext_ref, st_ref, y_ref,
                   row, reverse):
    q = CHUNK
    ext = ext_ref.at[2 * row + int(reverse)]
    ext[6:8, :] = halo_ref[row, 0, 0:2, :]
    ext[8:8 + q, :] = xbc_ref[row]
    ext[8 + q:10 + q, :] = halo_ref[row, 0, 2:4, :]
    conv = cb_ref[...] + cw_ref[0:1, :] * ext[6:6 + q, :]
    for kk in range(1, D_CONV):
        conv = conv + cw_ref[kk:kk + 1, :] * ext[6 + kk:6 + kk + q, :]
    act = _silu(conv)
    bm = act[:, D_SSD:D_SSD + LANE]
    cm = act[:, D_SSD + LANE:D_SSD + 2 * LANE]

    off = SSD_HEADS if reverse else 0
    raw = dt_ref[row] + dtb_ref[...]
    dt = jnp.maximum(raw, 0.0) + jnp.log1p(jnp.exp(-jnp.abs(raw)))
    dta = dt * (-jnp.exp(alog_ref[...]))
    ri = lax.broadcasted_iota(I32, (q, q), 0)
    ci = lax.broadcasted_iota(I32, (q, q), 1)
    mask = (ci >= ri) if reverse else (ci <= ri)
    cs = jnp.dot(mask.astype(F32), dta, precision=HIGHEST, preferred_element_type=F32)
    cs_t = cs.T
    dt_t = dt.T
    end = 0 if reverse else q - 1
    cs_end = cs[end:end + 1, :]
    wdec = jnp.exp(cs_end - cs) * dt
    ecs = jnp.exp(cs)
    cdec = jnp.exp(cs_end)

    lane = lax.broadcasted_iota(I32, (q, LANE), 1)
    sub = lax.broadcasted_iota(I32, (LANE, LANE), 0)
    cm16 = cm.astype(BF16)
    cbs = []
    for g in range(SSD_G):
        in_g = (lane >= g * SSD_N) & (lane < (g + 1) * SSD_N)
        cg = jnp.where(in_g, cm, 0.0).astype(BF16)
        cbs.append(lax.dot_general(cg, bm.astype(BF16), (((1,), (1,)), ((), ())), preferred_element_type=F32))

    for j in range(SSD_HEADS // 2):
        g = (2 * j) // (SSD_HEADS // SSD_G)
        xp = act[:, j * LANE:(j + 1) * LANE]
        xp16 = xp.astype(BF16)
        yd = []
        for hh in (2 * j, 2 * j + 1):
            ln = off + hh
            seg = cs[:, ln:ln + 1] - cs_t[ln:ln + 1, :]
            dec = jnp.exp(jnp.where(mask, seg, -jnp.inf))
            m = (cbs[g] * dec * dt_t[ln:ln + 1, :]).astype(BF16)
            yd.append(jnp.dot(m, xp16, preferred_element_type=F32))
        la, lb = off + 2 * j, off + 2 * j + 1
        first = lane < SSD_P
        y = jnp.where(first, yd[0], yd[1])
        st = st_ref[row, j]
        y_off = lax.dot_general(cm16, st.astype(BF16), (((1,), (1,)), ((), ())), preferred_element_type=F32)
        y = y + y_off * jnp.where(first, ecs[:, la:la + 1], ecs[:, lb:lb + 1])
        xw = (xp * jnp.where(first, wdec[:, la:la + 1], wdec[:, lb:lb + 1])).astype(BF16)
        in_g = (lane >= g * SSD_N) & (lane < (g + 1) * SSD_N)
        bg = jnp.where(in_g, bm, 0.0).astype(BF16)
        new = lax.dot_general(xw, bg, (((0,), (0,)), ((), ())), preferred_element_type=F32)
        st_ref[row, j] = st * jnp.where(sub < SSD_P, cdec[:, la:la + 1], cdec[:, lb:lb + 1]) + new
        if not reverse:
            y = y + dsk_ref[:, j * LANE:(j + 1) * LANE] * xp
        y_ref[row, :, j * LANE:(j + 1) * LANE] = y


def _ssd_kernel(xf_ref, hf_ref, dtf_ref, xb_ref, hb_ref, dtb_in_ref, cw_ref, cb_ref, alog_ref, dtbias_ref, dsk_ref,
                yf_ref, yb_ref, ext_ref, stf_ref, stb_ref):
    @pl.when(pl.program_id(1) == 0)
    def _():
        stf_ref[...] = jnp.zeros_like(stf_ref)
        stb_ref[...] = jnp.zeros_like(stb_ref)

    for row in range(xf_ref.shape[0]):
        _ssd_direction(xf_ref, hf_ref, dtf_ref, cw_ref, cb_ref, alog_ref, dtbias_ref, dsk_ref, ext_ref, stf_ref,
                       yf_ref, row, reverse=False)
        _ssd_direction(xb_ref, hb_ref, dtb_in_ref, cw_ref, cb_ref, alog_ref, dtbias_ref, dsk_ref, ext_ref, stb_ref,
                       yb_ref, row, reverse=True)


def _ssd(xbc, dt, conv_w8, conv_b, alog_flat, dtb_flat, dsk_flat, rows):
    b, s, _ = xbc.shape
    nc = s // CHUNK
    halo = _conv_halo(xbc, CHUNK)
    fwd = lambda n: pl.BlockSpec((rows, CHUNK, n), lambda i, c: (i, c, 0))
    bwd = lambda n: pl.BlockSpec((rows, CHUNK, n), lambda i, c: (i, nc - 1 - c, 0))
    hf = pl.BlockSpec((rows, 1, 8, CONV_DIM), lambda i, c: (i, c, 0, 0))
    hb = pl.BlockSpec((rows, 1, 8, CONV_DIM), lambda i, c: (i, nc - 1 - c, 0, 0))
    const = lambda a: pl.BlockSpec(a.shape, lambda i, c: (0, 0))
    return pl.pallas_call(
        _ssd_kernel,
        grid=(b // rows, nc),
        in_specs=[fwd(CONV_DIM), hf, fwd(LANE), bwd(CONV_DIM), hb, bwd(LANE),
                  const(conv_w8), const(conv_b), const(alog_flat), const(dtb_flat), const(dsk_flat)],
        out_specs=[fwd(D_SSD), bwd(D_SSD)],
        out_shape=[jax.ShapeDtypeStruct((b, s, D_SSD), F32)] * 2,
        scratch_shapes=[
            pltpu.VMEM((2 * rows, CHUNK + 16, CONV_DIM), F32),
            pltpu.VMEM((rows, SSD_HEADS // 2, LANE, LANE), F32),
            pltpu.VMEM((rows, SSD_HEADS // 2, LANE, LANE), F32),
        ],
        compiler_params=_cparams(("parallel", "arbitrary")),
        name="ssd",
    )(xbc, halo, dt, xbc, halo, dt, conv_w8, conv_b, alog_flat, dtb_flat, dsk_flat)


def _outproj_kernel(ya_ref, yf_ref, yb_ref, z_ref, x_ref, g1_ref, sc2_ref, sh2_ref, anw_ref, snw_ref, wout_ref,
                    n2_ref, wrt_ref, x1_ref, h2_ref, afft_ref):
    ya = _rms(ya_ref[0]) * anw_ref[...]
    y = (yf_ref[0] + yb_ref[0]) * _silu(z_ref[0])
    half = D_SSD // SSD_G
    snw = snw_ref[...]
    parts = [ya]
    for g in range(SSD_G):
        sl = slice(g * half, (g + 1) * half)
        parts.append(_rms(y[:, sl]) * snw[:, sl])
    ycat = jnp.concatenate(parts, axis=-1).astype(BF16)
    o = jnp.dot(ycat, wout_ref[...], preferred_element_type=F32)
    x1 = x_ref[0] + g1_ref[0] * o
    x1_ref[0] = x1
    h2 = _rms(x1) * n2_ref[...]
    h2 = h2 * (1.0 + sc2_ref[0]) + sh2_ref[0]
    h2_ref[0] = h2

    lt = lax.dot_general(wrt_ref[...], h2, (((1,), (1,)), ((), ())), precision=HIGHEST,
                         preferred_element_type=F32)
    et = jnp.exp(lt - jnp.max(lt, axis=0, keepdims=True))
    afft_ref[...] = et / jnp.sum(et, axis=0, keepdims=True)


def _outproj(ya, yf, yb, z, x, g1, sc2, sh2, attn_norm_w, ssd_norm_w, wout16, norm2_w, w_router, tm):
    b, s, d = x.shape
    t = b * s
    nt = s // tm
    tok = lambda n: pl.BlockSpec((1, tm, n), lambda i, j: (i, j, 0))
    per_b = pl.BlockSpec((1, 1, d), lambda i, j: (i, 0, 0))
    vec = lambda n: pl.BlockSpec((1, n), lambda i, j: (0, 0))
    full = lambda a: pl.BlockSpec(a.shape, lambda i, j: (0, 0))
    wrt = w_router.T
    return pl.pallas_call(
        _outproj_kernel,
        grid=(b, nt),
        in_specs=[tok(D_ATTN), tok(D_SSD), tok(D_SSD), tok(D_SSD), tok(d), per_b, per_b, per_b,
                  vec(D_ATTN), vec(D_SSD), full(wout16), vec(d), full(wrt)],
        out_specs=[tok(d), tok(d), pl.BlockSpec((N_EXPERTS, tm), lambda i, j: (0, i * nt + j))],
        out_shape=[
            jax.ShapeDtypeStruct((b, s, d), F32),
            jax.ShapeDtypeStruct((b, s, d), F32),
            jax.ShapeDtypeStruct((N_EXPERTS, t), F32),
        ],
        compiler_params=_cparams(("parallel", "arbitrary")),
        name="outproj",
    )(ya, yf, yb, z, x, g1, sc2, sh2, attn_norm_w.reshape(1, -1), ssd_norm_w.reshape(1, -1), wout16,
      norm2_w.reshape(1, d), wrt)


def _count(cond):
    c = jnp.sum(jnp.where(cond, 1.0, 0.0), axis=0, keepdims=True)
    return jnp.sum(c, axis=1, keepdims=True)


def _route_kernel(aff_ref, q_ref, idx_ref, starts_ref, *, cap, jc):
    a = aff_ref[0]
    nb = a.shape[0]
    keys = pltpu.bitcast(a, I32)
    kf = jnp.float32(cap)

    tau = jnp.zeros((1, 1), I32)
    for bit in range(30, -1, -1):
        cand = tau | jnp.int32(1 << bit)
        tau = jnp.where(_count(keys >= cand) >= kf, cand, tau)
    gt = keys > tau
    eq = keys == tau
    need = kf - _count(gt)

    r128 = lax.broadcasted_iota(I32, (LANE, LANE), 0)
    c128 = lax.broadcasted_iota(I32, (LANE, LANE), 1)
    upper = (r128 <= c128).astype(BF16)
    ones = jnp.ones((LANE, LANE), BF16)
    rb = lax.broadcasted_iota(I32, (nb, nb), 0)
    cb = lax.broadcasted_iota(I32, (nb, nb), 1)
    lower_strict = (cb < rb).astype(BF16)

    def prefix(m16):
        incl = jnp.dot(m16, upper, preferred_element_type=F32)
        cnt = jnp.dot(m16, ones, preferred_element_type=F32)
        bex = jnp.dot(lower_strict, cnt.astype(BF16), preferred_element_type=F32)
        return bex + incl - m16.astype(F32), cnt, bex

    gt16 = jnp.where(gt, 1.0, 0.0).astype(BF16)
    eq16 = jnp.where(eq, 1.0, 0.0).astype(BF16)
    gt_ex, _, _ = prefix(gt16)
    eq_ex, _, _ = prefix(eq16)
    sel = gt | (eq & (eq_ex < need))
    pos = gt_ex + jnp.minimum(eq_ex, need)
    q_ref[0] = jnp.where(sel, pos, -1.0).astype(I32)

    sel16 = jnp.where(sel, 1.0, 0.0).astype(BF16)
    _, cnt_c, bex_c = prefix(sel16)
    bin_c = bex_c + cnt_c
    lower_incl = (c128 <= r128).astype(BF16)
    incl_t = lax.dot_general(lower_incl, sel16, (((1,), (1,)), ((), ())), preferred_element_type=F32)
    cnt_r = lax.dot_general(jnp.ones((8, LANE), BF16), sel16, (((1,), (1,)), ((), ())),
                            preferred_element_type=F32)
    upper_strict = (rb < cb).astype(BF16)
    bex_r = jnp.dot(cnt_r.astype(BF16), upper_strict, preferred_element_type=F32)
    starts_ref[0] = bex_r[0:1, :].astype(I32)
    g_t = incl_t + bex_r[0:1, :]
    g_hi = jnp.floor(g_t * (1.0 / 64.0))
    g_lo = g_t - 64.0 * g_hi
    g_hi16 = g_hi.astype(BF16)
    g_lo16 = g_lo.astype(BF16)
    reps = jc // LANE
    bex_w = jnp.concatenate([bex_c] * reps, axis=1)
    bin_w = jnp.concatenate([bin_c] * reps, axis=1)
    lane_nb = lax.broadcasted_iota(I32, (nb, jc), 1).astype(F32)
    lane_128 = lax.broadcasted_iota(I32, (LANE, jc), 1).astype(F32)

    def chunk(c, carry):
        base = jnp.asarray(c * jc, I32).astype(F32)
        j_nb = lane_nb + base
        onehot_t = jnp.where((bex_w <= j_nb) & (j_nb < bin_w), 1.0, 0.0).astype(BF16)
        rows = (64.0 * jnp.dot(g_hi16, onehot_t, preferred_element_type=F32)
                + jnp.dot(g_lo16, onehot_t, preferred_element_type=F32))
        offs = jnp.sum(jnp.where(rows <= lane_128 + base, 1.0, 0.0), axis=0, keepdims=True)
        blk = jnp.sum(jnp.where(bin_w <= j_nb, 1.0, 0.0), axis=0, keepdims=True)
        idx_ref[0, :, pl.ds(pl.multiple_of(c * jc, jc), jc)] = (blk * float(LANE) + offs).astype(I32)
        return carry

    lax.fori_loop(0, cap // jc, chunk, 0)


def _route(aff_t, cap):
    e, t = aff_t.shape
    nb = t // LANE
    jc = min(256, cap)
    return pl.pallas_call(
        functools.partial(_route_kernel, cap=cap, jc=jc),
        grid=(e,),
        in_specs=[pl.BlockSpec((1, nb, LANE), lambda i: (i, 0, 0))],
        out_specs=[
            pl.BlockSpec((1, nb, LANE), lambda i: (i, 0, 0)),
            pl.BlockSpec((1, 1, cap), lambda i: (i, 0, 0)),
            pl.BlockSpec((1, 1, nb), lambda i: (i, 0, 0)),
        ],
        out_shape=[
            jax.ShapeDtypeStruct((e, nb, LANE), I32),
            jax.ShapeDtypeStruct((e, 1, cap), I32),
            jax.ShapeDtypeStruct((e, 1, nb), I32),
        ],
        compiler_params=_cparams(("arbitrary",)),
        name="route",
    )(aff_t.reshape(e, nb, LANE))


def _expert_kernel(idx_hbm, h_hbm, wg_ref, wu_ref, wd_ref, ye_ref, idx_smem, xbuf, x16, sem_idx, sem_rows, fence,
                   *, rm, nj, nsteps, fc):
    step = pl.program_id(0) * nj + pl.program_id(1)
    slot = step % 2

    def idx_copy(st):
        return pltpu.make_async_copy(idx_hbm.at[st % nsteps], idx_smem.at[st % 3], sem_idx.at[st % 3])

    def row_copy(st, sl, r):
        tok = idx_smem[st % 3, 0, r]
        return pltpu.make_async_copy(h_hbm.at[pl.ds(tok, 1)], xbuf.at[sl, pl.ds(r, 1)], sem_rows.at[sl])

    def rows_wait(sl):
        pltpu.make_async_copy(h_hbm.at[pl.ds(0, rm)], xbuf.at[sl], sem_rows.at[sl]).wait()

    @pl.when(step == 0)
    def _():
        first = idx_copy(0)
        first.start()
        first.wait()
        idx_copy(1).start()

        def issue(r, carry):
            row_copy(0, 0, r).start()
            return carry

        lax.fori_loop(0, rm, issue, 0)

    idx_copy(step + 1).wait()
    idx_copy(step + 2).start()
    rows_wait(slot)

    x16[...] = xbuf[slot].astype(BF16)
    nchunks = D_EXPERT // fc
    per = -(-rm // (nchunks - 1))
    acc = jnp.zeros((rm, D_MODEL), F32)
    for c in range(nchunks):
        for r in range(c * per, min(rm, (c + 1) * per)):
            row_copy(step + 1, 1 - slot, r).start()
        sl = slice(c * fc, (c + 1) * fc)
        g = jnp.dot(x16[...], wg_ref[0, :, sl], preferred_element_type=F32)
        u = jnp.dot(x16[...], wu_ref[0, :, sl], preferred_element_type=F32)
        hid = (_silu(g) * u).astype(BF16)
        acc = acc + jnp.dot(hid, wd_ref[0, sl, :], preferred_element_type=F32)
        pl.semaphore_signal(fence, 1)
        pl.semaphore_wait(fence, 1)
    ye_ref[...] = acc.astype(BF16)

    @pl.when(step == nsteps - 1)
    def _():
        idx_copy(step + 2).wait()
        rows_wait(1 - slot)


def _experts(idx, h2, wg16, wu16, wd16, rm):
    e, _, cap = idx.shape
    nj = cap // rm
    t, d = h2.shape
    idx_steps = idx.reshape(e * nj, 1, rm)
    return pl.pallas_call(
        functools.partial(_expert_kernel, rm=rm, nj=nj, nsteps=e * nj, fc=512),
        grid=(e, nj),
        in_specs=[
            pl.BlockSpec(memory_space=pl.ANY),
            pl.BlockSpec(memory_space=pl.ANY),
            pl.BlockSpec((1, d, D_EXPERT), lambda i, j: (i, 0, 0)),
            pl.BlockSpec((1, d, D_EXPERT), lambda i, j: (i, 0, 0)),
            pl.BlockSpec((1, D_EXPERT, d), lambda i, j: (i, 0, 0)),
        ],
        out_specs=pl.BlockSpec((rm, d), lambda i, j: (i * nj + j, 0)),
        out_shape=jax.ShapeDtypeStruct((e * cap, d), BF16),
        scratch_shapes=[
            pltpu.SMEM((3, 1, rm), I32),
            pltpu.VMEM((2, rm, d), F32),
            pltpu.VMEM((rm, d), BF16),
            pltpu.SemaphoreType.DMA((3,)),
            pltpu.SemaphoreType.DMA((2,)),
            pltpu.SemaphoreType.REGULAR,
        ],
        compiler_params=_cparams(("arbitrary", "arbitrary")),
        name="expert",
    )(idx_steps, h2, wg16, wu16, wd16)


def _combine_kernel(starts_ref, q_ref, aff_ref, x1_ref, g2_ref, fnw_ref, ye_hbm, y_ref, ybuf, acc_ref, sem,
                    *, cap, tm, kt):
    i = pl.program_id(0)
    n = pl.num_programs(0)
    slot = i % 2
    ch = BF16_ROWS

    def plan(tile):
        firsts, counts, bases = [], [], []
        packed = jnp.int32(0)
        for e in range(N_EXPERTS):
            s0 = starts_ref[tile * N_EXPERTS + e]
            s1 = starts_ref[(tile + 1) * N_EXPERTS + e]
            gs = e * cap + s0
            a = (gs // ch) * ch
            cnt = jnp.where(s1 > s0, (gs + (s1 - s0) - a + ch - 1) // ch, 0)
            firsts.append(a)
            counts.append(cnt)
            bases.append(packed)
            packed = packed + cnt * ch
        return firsts, counts, bases, packed

    def chunk_copy(src, dst, sl):
        return pltpu.make_async_copy(ye_hbm.at[pl.ds(pl.multiple_of(src, ch), ch)],
                                     ybuf.at[sl, pl.ds(pl.multiple_of(dst, ch), ch)], sem.at[sl])

    def start_tile(tile, sl):
        firsts, counts, bases, _ = plan(tile)
        for e in range(N_EXPERTS):
            def issue(k, carry, e=e):
                chunk_copy(firsts[e] + k * ch, bases[e] + k * ch, sl).start()
                return carry

            lax.fori_loop(0, counts[e], issue, 0)

    def wait_tile(tile, sl):
        _, _, _, packed = plan(tile)

        def wait_one(k, carry):
            chunk_copy(0, 0, sl).wait()
            return carry

        lax.fori_loop(0, packed // ch, wait_one, 0)

    @pl.when(i == 0)
    def _():
        ybuf[...] = jnp.zeros_like(ybuf)
        start_tile(0, 0)

    @pl.when(i + 1 < n)
    def _():
        start_tile(i + 1, 1 - slot)

    wait_tile(i, slot)

    firsts, _, bases, packed = plan(i)
    q = q_ref[...]
    aff = aff_ref[...]
    expert = lax.broadcasted_iota(I32, (N_EXPERTS, tm), 0)
    shift = jnp.zeros((N_EXPERTS, tm), I32)
    for e in range(N_EXPERTS):
        shift = jnp.where(expert == e, e * cap - firsts[e] + bases[e], shift)
    row = jnp.where(q >= 0, q + shift, jnp.int32(-1))
    acc_ref[...] = jnp.zeros_like(acc_ref)

    def expand(c, carry):
        packed_row = lax.broadcasted_iota(I32, (kt, tm), 0) + c * kt
        val_t = jnp.zeros((kt, tm), F32)
        for e in range(N_EXPERTS):
            val_t = jnp.where(row[e:e + 1, :] == packed_row, aff[e:e + 1, :], val_t)
        val = val_t.T
        hi = val.astype(BF16)
        lo = (val - hi.astype(F32)).astype(BF16)
        rows = ybuf[slot, pl.ds(pl.multiple_of(c * kt, kt), kt), :]
        acc_ref[...] += jnp.dot(jnp.concatenate([hi, lo], axis=0), rows, preferred_element_type=F32)
        return carry

    lax.fori_loop(0, (packed + kt - 1) // kt, expand, 0)

    out = x1_ref[...] + g2_ref[0] * (acc_ref[0:tm, :] + acc_ref[tm:2 * tm, :])
    y_ref[...] = _rms(out) * fnw_ref[...]


def _combine(starts_flat, q_slot, aff_t, x1, g2, final_norm_w, ye, cap, tm, tiles_per_batch):
    t, d = x1.shape
    kt = 2 * LANE
    chunks_per_expert = (tm + 2 * (BF16_ROWS - 1)) // BF16_ROWS
    packed_max = -(-N_EXPERTS * chunks_per_expert * BF16_ROWS // kt) * kt
    grid_spec = pltpu.PrefetchScalarGridSpec(
        num_scalar_prefetch=1,
        grid=(t // tm,),
        in_specs=[
            pl.BlockSpec((N_EXPERTS, tm), lambda i, s: (0, i)),
            pl.BlockSpec((N_EXPERTS, tm), lambda i, s: (0, i)),
            pl.BlockSpec((tm, d), lambda i, s: (i, 0)),
            pl.BlockSpec((1, 1, d), lambda i, s: (i // tiles_per_batch, 0, 0)),
            pl.BlockSpec((1, d), lambda i, s: (0, 0)),
            pl.BlockSpec(memory_space=pl.ANY),
        ],
        out_specs=pl.BlockSpec((tm, d), lambda i, s: (i, 0)),
        scratch_shapes=[
            pltpu.VMEM((2, packed_max, d), BF16),
            pltpu.VMEM((2 * tm, d), F32),
            pltpu.SemaphoreType.DMA((2,)),
        ],
    )
    return pl.pallas_call(
        functools.partial(_combine_kernel, cap=cap, tm=tm, kt=kt),
        grid_spec=grid_spec,
        out_shape=jax.ShapeDtypeStruct((t, d), F32),
        compiler_params=_cparams(("arbitrary",)),
        name="combine",
    )(starts_flat, q_slot, aff_t, x1, g2, final_norm_w.reshape(1, d), ye)


def _rotate_half_cols(w):
    half = ROPE // 2
    return jnp.concatenate([-w[..., half:], w[..., :half]], axis=-1)


def _prep_weights(w_in, w_uq, w_ukv, w_out, w_gate, w_up, w_down):
    d = w_in.shape[0]
    offs = [Q_RANK, Q_RANK + KV_RANK, Q_RANK + KV_RANK + ROPE, Q_RANK + KV_RANK + ROPE + D_SSD,
            Q_RANK + KV_RANK + ROPE + D_SSD + CONV_DIM]
    w_cq, w_ckv, w_kr, w_z, w_xbc, w_dt = jnp.split(w_in, offs, axis=1)
    place = lambda w: jnp.pad(w, ((0, 0), (NOPE, HEAD_PAD - NOPE - ROPE)))
    win_ext = jnp.concatenate(
        [w_cq, w_ckv, place(w_kr), place(_rotate_half_cols(w_kr)), w_z, w_xbc,
         jnp.pad(w_dt, ((0, 0), (0, LANE - w_dt.shape[1])))], axis=1).astype(BF16)
    assert win_ext.shape == (d, _N_IN)

    wq = w_uq.reshape(Q_RANK, N_HEADS, NOPE + ROPE)
    pad_h = lambda w: jnp.pad(w, ((0, 0), (0, 0), (0, HEAD_PAD - NOPE - ROPE))).reshape(Q_RANK, N_HEADS * HEAD_PAD)
    wq_rot = jnp.concatenate([jnp.zeros_like(wq[..., :NOPE]), _rotate_half_cols(wq[..., NOPE:])], axis=-1)
    wq_ext = jnp.concatenate([pad_h(wq), pad_h(wq_rot)], axis=1).astype(BF16)

    wkv = w_ukv.reshape(KV_RANK, N_HEADS, NOPE + V_DIM)
    wk = jnp.pad(wkv[..., :NOPE], ((0, 0), (0, 0), (0, HEAD_PAD - NOPE))).reshape(KV_RANK, N_HEADS * HEAD_PAD)
    wv = jnp.pad(wkv[..., NOPE:], ((0, 0), (0, 0), (0, HEAD_PAD - V_DIM))).reshape(KV_RANK, N_HEADS * HEAD_PAD)
    wkv_ext = jnp.concatenate([wk, wv], axis=1).astype(BF16)
    return win_ext, wq_ext, wkv_ext, w_out.astype(BF16), w_gate.astype(BF16), w_up.astype(BF16), w_down.astype(BF16)


def _rope_table(seq):
    pos = jnp.arange(seq, dtype=F32)
    inv = 1.0 / (ROPE_THETA ** (jnp.arange(0, ROPE, 2, dtype=F32) / ROPE))
    ang = pos[:, None] * inv[None, :]
    cos = jnp.concatenate([jnp.cos(ang)] * 2, axis=-1)
    sin = jnp.concatenate([jnp.sin(ang)] * 2, axis=-1)
    zpad = jnp.zeros((seq, HEAD_PAD - NOPE - ROPE), F32)
    qs = (NOPE + ROPE) ** -0.5 * math.log2(math.e)
    cosq = jnp.concatenate([jnp.ones((seq, NOPE), F32), cos, zpad], axis=-1) * qs
    sinq = jnp.concatenate([jnp.zeros((seq, NOPE), F32), sin, zpad], axis=-1) * qs
    cosk = jnp.concatenate([jnp.zeros((seq, NOPE), F32), cos, zpad], axis=-1)
    sink = jnp.concatenate([jnp.zeros((seq, NOPE), F32), sin, zpad], axis=-1)
    return jnp.concatenate([cosq, sinq, cosk, sink], axis=-1)


def _conv_halo(xbc, tile):
    b, s, c = xbc.shape
    nc = s // tile
    r = xbc.reshape(b, nc, tile, c)
    z2 = jnp.zeros((b, 1, 2, c), xbc.dtype)
    prev = jnp.concatenate([z2, r[:, :-1, tile - 2:, :]], axis=1)
    nxt = jnp.concatenate([r[:, 1:, :2, :], z2], axis=1)
    return jnp.concatenate([prev, nxt, jnp.zeros((b, nc, 4, c), xbc.dtype)], axis=2)


def _trunk(x, mod, p):
    b, s, d = x.shape
    t = b * s
    cap = max(1, CAP_FACTOR * t // N_EXPERTS)
    sh1, sc1, g1, sh2, sc2, g2 = [m.reshape(b, 1, d) for m in jnp.split(mod, 6, axis=-1)]

    tm = min(512, s)
    q, k, v, z, xbc, dt = _inproj(x, sc1, sh1, p["norm1_w"], p["win_ext"], p["q_norm_w"], p["wq_ext"],
                                  p["kv_norm_w"], p["wkv_ext"], _rope_table(s), tm)
    ya = _attention(q, k, v, min(s, max(LANE, _SCORE_ELEMS // s)))
    yf, yb = _ssd(xbc, dt, p["conv_w8"], p["conv_b"], p["alog_flat"], p["dtb_flat"], p["dsk_flat"],
                  2 if b % 2 == 0 else 1)
    x1, h2, aff_t = _outproj(ya, yf, yb, z, x, g1, sc2, sh2, p["attn_norm_w"], p["ssd_norm_w"], p["wout16"],
                             p["norm2_w"], p["w_router"], tm)
    q_slot, idx, starts = _route(aff_t, cap)
    ye = _experts(idx, h2.reshape(t, d), p["wg16"], p["wu16"], p["wd16"], min(512, cap))
    starts_flat = jnp.concatenate([starts.reshape(N_EXPERTS, t // LANE).T,
                                   jnp.full((1, N_EXPERTS), cap, I32)], axis=0).reshape(-1)
    y = _combine(starts_flat, q_slot.reshape(N_EXPERTS, t), aff_t, x1.reshape(t, d), g2, p["final_norm_w"], ye,
                 cap, LANE, s // LANE)
    return y.reshape(b, s, d)


def kernel(x_prompt, x_sample, c_prompt, c_sample, w_ada, b_ada, norm1_w, w_in, q_norm_w, w_uq, kv_norm_w, w_ukv,
           attn_norm_w, conv_w, conv_b, a_log, dt_bias, d_skip, ssd_norm_w, w_out, norm2_w, w_router, w_gate, w_up,
           w_down, final_norm_w):
    win_ext, wq_ext, wkv_ext, wout16, wg16, wu16, wd16 = _prep_weights(
        w_in[0], w_uq[0], w_ukv[0], w_out[0], w_gate[0], w_up[0], w_down[0])
    flat16 = lambda a: jnp.pad(a.reshape(1, 2 * SSD_HEADS), ((0, 0), (0, LANE - 2 * SSD_HEADS)))
    p = dict(
        norm1_w=norm1_w[0], win_ext=win_ext, q_norm_w=q_norm_w[0], wq_ext=wq_ext, kv_norm_w=kv_norm_w[0],
        wkv_ext=wkv_ext, attn_norm_w=attn_norm_w[0], ssd_norm_w=ssd_norm_w[0], wout16=wout16, norm2_w=norm2_w[0],
        w_router=w_router[0], wg16=wg16, wu16=wu16, wd16=wd16, final_norm_w=final_norm_w,
        conv_w8=jnp.pad(conv_w[0], ((0, 8 - D_CONV), (0, 0))), conv_b=conv_b[0].reshape(1, CONV_DIM),
        alog_flat=flat16(a_log[0]), dtb_flat=flat16(dt_bias[0]),
        dsk_flat=jnp.repeat(d_skip[0], SSD_P).reshape(1, D_SSD),
    )
    nbp = c_prompt.shape[0]
    mod = _mod(jnp.concatenate([c_prompt, c_sample], axis=0), w_ada[0], b_ada[0])
    return _trunk(x_prompt, mod[:nbp], p), _trunk(x_sample, mod[nbp:], p)
```

```python
import functools
import math

import jax
import jax.numpy as jnp
from jax import lax
from jax.experimental import pallas as pl
from jax.experimental.pallas import tpu as pltpu

F32 = jnp.float32
BF16 = jnp.bfloat16
I32 = jnp.int32
HIGHEST = lax.Precision.HIGHEST

D_MODEL = 1024
N_HEADS = 8
NOPE = 64
ROPE = 32
V_DIM = 64
Q_RANK = 256
KV_RANK = 128
ROPE_THETA = 10000.0
D_ATTN = N_HEADS * V_DIM
SSD_HEADS = 8
SSD_P = 64
D_SSD = SSD_HEADS * SSD_P
SSD_G = 2
SSD_N = 64
D_CONV = 5
CHUNK = 128
CONV_DIM = D_SSD + 2 * SSD_G * SSD_N
N_EXPERTS = 16
CAP_FACTOR = 2
D_EXPERT = 2048
EPS = 1e-6

LANE = 128
HEAD_PAD = 128
BF16_ROWS = 16
VMEM_LIMIT = 56 * 1024 * 1024
_SCORE_ELEMS = 1 << 20

_C_CQ = 0
_C_CKV = _C_CQ + Q_RANK
_C_KR = _C_CKV + KV_RANK
_C_KRR = _C_KR + LANE
_C_Z = _C_KRR + LANE
_C_XBC = _C_Z + D_SSD
_C_DT = _C_XBC + CONV_DIM
_N_IN = _C_DT + LANE


def _cparams(sem, vmem=VMEM_LIMIT):
    return pltpu.CompilerParams(dimension_semantics=sem, vmem_limit_bytes=vmem)


def _silu(x):
    return x / (1.0 + jnp.exp(-x))


def _rms(x):
    return x * lax.rsqrt(jnp.mean(x * x, axis=-1, keepdims=True) + EPS)


def _mod_kernel(c_ref, w_ref, b_ref, o_ref):
    a = _silu(c_ref[...])
    o_ref[...] = jnp.dot(a, w_ref[...], precision=HIGHEST, preferred_element_type=F32) + b_ref[...]


def _mod(c, w_ada, b_ada):
    nb, d = c.shape
    n = w_ada.shape[1]
    tn = 1536
    return pl.pallas_call(
        _mod_kernel,
        grid=(n // tn,),
        in_specs=[
            pl.BlockSpec((nb, d), lambda j: (0, 0)),
            pl.BlockSpec((d, tn), lambda j: (0, j)),
            pl.BlockSpec((1, tn), lambda j: (0, j)),
        ],
        out_specs=pl.BlockSpec((nb, tn), lambda j: (0, j)),
        out_shape=jax.ShapeDtypeStruct((nb, n), F32),
        compiler_params=_cparams(("arbitrary",)),
        name="mod",
    )(c, w_ada, b_ada.reshape(1, n))


def _inproj_kernel(x_ref, sc_ref, sh_ref, n1_ref, win_ref, qn_ref, wq_ref, kvn_ref, wkv_ref, tab_ref,
                   q_ref, k_ref, v_ref, z_ref, xbc_ref, dt_ref):
    h = _rms(x_ref[0]) * n1_ref[...]
    h = h * (1.0 + sc_ref[0]) + sh_ref[0]
    proj = jnp.dot(h.astype(BF16), win_ref[...], preferred_element_type=F32)
    z_ref[0] = proj[:, _C_Z:_C_XBC]
    xbc_ref[0] = proj[:, _C_XBC:_C_DT]
    dt_ref[0] = proj[:, _C_DT:_N_IN]

    tab = tab_ref[...]
    cosq, sinq = tab[:, 0:LANE], tab[:, LANE:2 * LANE]
    cosk, sink = tab[:, 2 * LANE:3 * LANE], tab[:, 3 * LANE:4 * LANE]

    cqn = (_rms(proj[:, _C_CQ:_C_CKV]) * qn_ref[...]).astype(BF16)
    q2 = jnp.dot(cqn, wq_ref[...], preferred_element_type=F32)
    ckvn = (_rms(proj[:, _C_CKV:_C_KR]) * kvn_ref[...]).astype(BF16)
    kv = jnp.dot(ckvn, wkv_ref[...], preferred_element_type=F32)
    krope = proj[:, _C_KR:_C_KRR] * cosk + proj[:, _C_KRR:_C_Z] * sink
    nq = N_HEADS * HEAD_PAD
    is_v = lax.broadcasted_iota(I32, (x_ref.shape[1], HEAD_PAD), 1) < V_DIM
    for hd in range(N_HEADS):
        sl = slice(hd * HEAD_PAD, (hd + 1) * HEAD_PAD)
        rot = slice(nq + hd * HEAD_PAD, nq + (hd + 1) * HEAD_PAD)
        q_ref[0, hd] = (q2[:, sl] * cosq + q2[:, rot] * sinq).astype(BF16)
        k_ref[0, hd] = (kv[:, sl] + krope).astype(BF16)
        v_ref[0, hd] = jnp.where(is_v, kv[:, rot], 1.0).astype(BF16)


def _inproj(x, sc1, sh1, norm1_w, win_ext, q_norm_w, wq_ext, kv_norm_w, wkv_ext, tab, tm):
    b, s, d = x.shape
    vec = lambda n: pl.BlockSpec((1, n), lambda i, j: (0, 0))
    full = lambda a: pl.BlockSpec(a.shape, lambda i, j: (0, 0))
    per_b = pl.BlockSpec((1, 1, d), lambda i, j: (i, 0, 0))
    tok = lambda n: pl.BlockSpec((1, tm, n), lambda i, j: (i, j, 0))
    heads = pl.BlockSpec((1, N_HEADS, tm, HEAD_PAD), lambda i, j: (i, 0, j, 0))
    head_shape = jax.ShapeDtypeStruct((b, N_HEADS, s, HEAD_PAD), BF16)
    return pl.pallas_call(
        _inproj_kernel,
        grid=(b, s // tm),
        in_specs=[tok(d), per_b, per_b, vec(d), full(win_ext), vec(Q_RANK), full(wq_ext), vec(KV_RANK),
                  full(wkv_ext), pl.BlockSpec((tm, 4 * LANE), lambda i, j: (j, 0))],
        out_specs=[heads, heads, heads, tok(D_SSD), tok(CONV_DIM), tok(LANE)],
        out_shape=[
            head_shape, head_shape, head_shape,
            jax.ShapeDtypeStruct((b, s, D_SSD), F32),
            jax.ShapeDtypeStruct((b, s, CONV_DIM), F32),
            jax.ShapeDtypeStruct((b, s, LANE), F32),
        ],
        compiler_params=_cparams(("parallel", "arbitrary")),
        name="inproj",
    )(x, sc1, sh1, norm1_w.reshape(1, d), win_ext, q_norm_w.reshape(1, -1), wq_ext,
      kv_norm_w.reshape(1, -1), wkv_ext, tab)


def _attn_kernel(q_ref, k_ref, v_ref, o_ref, s_scr):
    lane = lax.broadcasted_iota(I32, (q_ref.shape[2], HEAD_PAD), 1)
    first = lane < V_DIM

    def scores(h):
        s = lax.dot_general(q_ref[0, h], k_ref[0, h], (((1,), (1,)), ((), ())), preferred_element_type=F32)
        s_scr[h % 2] = s
        return jnp.max(s, axis=-1, keepdims=True)

    def finish(h, m):
        p = jnp.exp2(s_scr[h % 2] - m).astype(BF16)
        o = jnp.dot(p, v_ref[0, h], preferred_element_type=F32)
        return o / pltpu.roll(o, V_DIM, axis=1)

    def emit(h, o, prev):
        if h % 2 == 0:
            return o
        pair = h // 2
        o_ref[0, :, pair * HEAD_PAD:(pair + 1) * HEAD_PAD] = jnp.where(first, prev, pltpu.roll(o, V_DIM, axis=1))
        return None

    m_prev = scores(0)
    held = None
    for h in range(1, N_HEADS):
        m = scores(h)
        held = emit(h - 1, finish(h - 1, m_prev), held)
        m_prev = m
    emit(N_HEADS - 1, finish(N_HEADS - 1, m_prev), held)


def _attention(q, k, v, tq):
    b, _, s, _ = q.shape
    kv_spec = pl.BlockSpec((1, N_HEADS, s, HEAD_PAD), lambda i, j: (i, 0, 0, 0))
    return pl.pallas_call(
        _attn_kernel,
        grid=(b, s // tq),
        in_specs=[pl.BlockSpec((1, N_HEADS, tq, HEAD_PAD), lambda i, j: (i, 0, j, 0)), kv_spec, kv_spec],
        out_specs=pl.BlockSpec((1, tq, D_ATTN), lambda i, j: (i, j, 0)),
        out_shape=jax.ShapeDtypeStruct((b, s, D_ATTN), F32),
        scratch_shapes=[pltpu.VMEM((2, tq, s), F32)],
        compiler_params=_cparams(("parallel", "arbitrary")),
        name="attn",
    )(q, k, v)


def _ssd_direction(xbc_ref, halo_ref, dt_ref, cw_ref, cb_ref, alog_ref, dtb_ref, dsk_ref, ext_ref, st_ref, y_ref,
                   row, reverse):
    q = CHUNK
    ext = ext_ref.at[2 * row + int(reverse)]
    ext[6:8, :] = halo_ref[row, 0, 0:2, :]
    ext[8:8 + q, :] = xbc_ref[row]
    ext[8 + q:10 + q, :] = halo_ref[row, 0, 2:4, :]
    conv = cb_ref[...] + cw_ref[0:1, :] * ext[6:6 + q, :]
    for kk in range(1, D_CONV):
        conv = conv + cw_ref[kk:kk + 1, :] * ext[6 + kk:6 + kk + q, :]
    act = _silu(conv)
    bm = act[:, D_SSD:D_SSD + LANE]
    cm = act[:, D_SSD + LANE:D_SSD + 2 * LANE]

    off = SSD_HEADS if reverse else 0
    raw = dt_ref[row] + dtb_ref[...]
    dt = jnp.maximum(raw, 0.0) + jnp.log1p(jnp.exp(-jnp.abs(raw)))
    dta = dt * (-jnp.exp(alog_ref[...]))
    ri = lax.broadcasted_iota(I32, (q, q), 0)
    ci = lax.broadcasted_iota(I32, (q, q), 1)
    mask = (ci >= ri) if reverse else (ci <= ri)
    cs = jnp.dot(mask.astype(F32), dta, precision=HIGHEST, preferred_element_type=F32)
    cs_t = cs.T
    dt_t = dt.T
    end = 0 if reverse else q - 1
    cs_end = cs[end:end + 1, :]
    wdec = jnp.exp(cs_end - cs) * dt
    ecs = jnp.exp(cs)
    cdec = jnp.exp(cs_end)

    lane = lax.broadcasted_iota(I32, (q, LANE), 1)
    sub = lax.broadcasted_iota(I32, (LANE, LANE), 0)
    cm16 = cm.astype(BF16)
    cbs = []
    for g in range(SSD_G):
        in_g = (lane >= g * SSD_N) & (lane < (g + 1) * SSD_N)
        cg = jnp.where(in_g, cm, 0.0).astype(BF16)
        cbs.append(lax.dot_general(cg, bm.astype(BF16), (((1,), (1,)), ((), ())), preferred_element_type=F32))

    for j in range(SSD_HEADS // 2):
        g = (2 * j) // (SSD_HEADS // SSD_G)
        xp = act[:, j * LANE:(j + 1) * LANE]
        xp16 = xp.astype(BF16)
        yd = []
        for hh in (2 * j, 2 * j + 1):
            ln = off + hh
            seg = cs[:, ln:ln + 1] - cs_t[ln:ln + 1, :]
            dec = jnp.exp(jnp.where(mask, seg, -jnp.inf))
            m = (cbs[g] * dec * dt_t[ln:ln + 1, :]).astype(BF16)
            yd.append(jnp.dot(m, xp16, preferred_element_type=F32))
        la, lb = off + 2 * j, off + 2 * j + 1
        first = lane < SSD_P
        y = jnp.where(first, yd[0], yd[1])
        st = st_ref[row, j]
        y_off = lax.dot_general(cm16, st.astype(BF16), (((1,), (1,)), ((), ())), preferred_element_type=F32)
        y = y + y_off * jnp.where(first, ecs[:, la:la + 1], ecs[:, lb:lb + 1])
        xw = (xp * jnp.where(first, wdec[:, la:la + 1], wdec[:, lb:lb + 1])).astype(BF16)
        in_g = (lane >= g * SSD_N) & (lane < (g + 1) * SSD_N)
        bg = jnp.where(in_g, bm, 0.0).astype(BF16)
        new = lax.dot_general(xw, bg, (((0,), (0,)), ((), ())), preferred_element_type=F32)
        st_ref[row, j] = st * jnp.where(sub < SSD_P, cdec[:, la:la + 1], cdec[:, lb:lb + 1]) + new
        if not reverse:
            y = y + dsk_ref[:, j * LANE:(j + 1) * LANE] * xp
        y_ref[row, :, j * LANE:(j + 1) * LANE] = y


def _ssd_kernel(xf_ref, hf_ref, dtf_ref, xb_ref, hb_ref, dtb_in_ref, cw_ref, cb_ref, alog_ref, dtbias_ref, dsk_ref,
                yf_ref, yb_ref, ext_ref, stf_ref, stb_ref):
    @pl.when(pl.program_id(1) == 0)
    def _():
        stf_ref[...] = jnp.zeros_like(stf_ref)
        stb_ref[...] = jnp.zeros_like(stb_ref)

    for row in range(xf_ref.shape[0]):
        _ssd_direction(xf_ref, hf_ref, dtf_ref, cw_ref, cb_ref, alog_ref, dtbias_ref, dsk_ref, ext_ref, stf_ref,
                       yf_ref, row, reverse=False)
        _ssd_direction(xb_ref, hb_ref, dtb_in_ref, cw_ref, cb_ref, alog_ref, dtbias_ref, dsk_ref, ext_ref, stb_ref,
                       yb_ref, row, reverse=True)


def _ssd(xbc, dt, conv_w8, conv_b, alog_flat, dtb_flat, dsk_flat, rows):
    b, s, _ = xbc.shape
    nc = s // CHUNK
    halo = _conv_halo(xbc, CHUNK)
    fwd = lambda n: pl.BlockSpec((rows, CHUNK, n), lambda i, c: (i, c, 0))
    bwd = lambda n: pl.BlockSpec((rows, CHUNK, n), lambda i, c: (i, nc - 1 - c, 0))
    hf = pl.BlockSpec((rows, 1, 8, CONV_DIM), lambda i, c: (i, c, 0, 0))
    hb = pl.BlockSpec((rows, 1, 8, CONV_DIM), lambda i, c: (i, nc - 1 - c, 0, 0))
    const = lambda a: pl.BlockSpec(a.shape, lambda i, c: (0, 0))
    return pl.pallas_call(
        _ssd_kernel,
        grid=(b // rows, nc),
        in_specs=[fwd(CONV_DIM), hf, fwd(LANE), bwd(CONV_DIM), hb, bwd(LANE),
                  const(conv_w8), const(conv_b), const(alog_flat), const(dtb_flat), const(dsk_flat)],
        out_specs=[fwd(D_SSD), bwd(D_SSD)],
        out_shape=[jax.ShapeDtypeStruct((b, s, D_SSD), F32)] * 2,
        scratch_shapes=[
            pltpu.VMEM((2 * rows, CHUNK + 16, CONV_DIM), F32),
            pltpu.VMEM((rows, SSD_HEADS // 2, LANE, LANE), F32),
            pltpu.VMEM((rows, SSD_HEADS // 2, LANE, LANE), F32),
        ],
        compiler_params=_cparams(("parallel", "arbitrary")),
        name="ssd",
    )(xbc, halo, dt, xbc, halo, dt, conv_w8, conv_b, alog_flat, dtb_flat, dsk_flat)


def _outproj_kernel(ya_ref, yf_ref, yb_ref, z_ref, x_ref, g1_ref, sc2_ref, sh2_ref, anw_ref, snw_ref, wout_ref,
                    n2_ref, wrt_ref, x1_ref, h2_ref, afft_ref):
    ya = _rms(ya_ref[0]) * anw_ref[...]
    y = (yf_ref[0] + yb_ref[0]) * _silu(z_ref[0])
    half = D_SSD // SSD_G
    snw = snw_ref[...]
    parts = [ya]
    for g in range(SSD_G):
        sl = slice(g * half, (g + 1) * half)
        parts.append(_rms(y[:, sl]) * snw[:, sl])
    ycat = jnp.concatenate(parts, axis=-1).astype(BF16)
    o = jnp.dot(ycat, wout_ref[...], preferred_element_type=F32)
    x1 = x_ref[0] + g1_ref[0] * o
    x1_ref[0] = x1
    h2 = _rms(x1) * n2_ref[...]
    h2 = h2 * (1.0 + sc2_ref[0]) + sh2_ref[0]
    h2_ref[0] = h2

    lt = lax.dot_general(wrt_ref[...], h2, (((1,), (1,)), ((), ())), precision=HIGHEST,
                         preferred_element_type=F32)
    et = jnp.exp(lt - jnp.max(lt, axis=0, keepdims=True))
    afft_ref[...] = et / jnp.sum(et, axis=0, keepdims=True)


def _outproj(ya, yf, yb, z, x, g1, sc2, sh2, attn_norm_w, ssd_norm_w, wout16, norm2_w, w_router, tm):
    b, s, d = x.shape
    t = b * s
    nt = s // tm
    tok = lambda n: pl.BlockSpec((1, tm, n), lambda i, j: (i, j, 0))
    per_b = pl.BlockSpec((1, 1, d), lambda i, j: (i, 0, 0))
    vec = lambda n: pl.BlockSpec((1, n), lambda i, j: (0, 0))
    full = lambda a: pl.BlockSpec(a.shape, lambda i, j: (0, 0))
    wrt = w_router.T
    return pl.pallas_call(
        _outproj_kernel,
        grid=(b, nt),
        in_specs=[tok(D_ATTN), tok(D_SSD), tok(D_SSD), tok(D_SSD), tok(d), per_b, per_b, per_b,
                  vec(D_ATTN), vec(D_SSD), full(wout16), vec(d), full(wrt)],
        out_specs=[tok(d), tok(d), pl.BlockSpec((N_EXPERTS, tm), lambda i, j: (0, i * nt + j))],
        out_shape=[
            jax.ShapeDtypeStruct((b, s, d), F32),
            jax.ShapeDtypeStruct((b, s, d), F32),
            jax.ShapeDtypeStruct((N_EXPERTS, t), F32),
        ],
        compiler_params=_cparams(("parallel", "arbitrary")),
        name="outproj",
    )(ya, yf, yb, z, x, g1, sc2, sh2, attn_norm_w.reshape(1, -1), ssd_norm_w.reshape(1, -1), wout16,
      norm2_w.reshape(1, d), wrt)


def _count(cond):
    c = jnp.sum(jnp.where(cond, 1.0, 0.0), axis=0, keepdims=True)
    return jnp.sum(c, axis=1, keepdims=True)


def _route_kernel(aff_ref, q_ref, idx_ref, starts_ref, *, cap, jc):
    a = aff_ref[0]
    nb = a.shape[0]
    keys = pltpu.bitcast(a, I32)
    kf = jnp.float32(cap)

    tau = jnp.zeros((1, 1), I32)
    for bit in range(30, -1, -1):
        cand = tau | jnp.int32(1 << bit)
        tau = jnp.where(_count(keys >= cand) >= kf, cand, tau)
    gt = keys > tau
    eq = keys == tau
    need = kf - _count(gt)

    r128 = lax.broadcasted_iota(I32, (LANE, LANE), 0)
    c128 = lax.broadcasted_iota(I32, (LANE, LANE), 1)
    upper = (r128 <= c128).astype(BF16)
    ones = jnp.ones((LANE, LANE), BF16)
    rb = lax.broadcasted_iota(I32, (nb, nb), 0)
    cb = lax.broadcasted_iota(I32, (nb, nb), 1)
    lower_strict = (cb < rb).astype(BF16)

    def prefix(m16):
        incl = jnp.dot(m16, upper, preferred_element_type=F32)
        cnt = jnp.dot(m16, ones, preferred_element_type=F32)
        bex = jnp.dot(lower_strict, cnt.astype(BF16), preferred_element_type=F32)
        return bex + incl - m16.astype(F32), cnt, bex

    gt16 = jnp.where(gt, 1.0, 0.0).astype(BF16)
    eq16 = jnp.where(eq, 1.0, 0.0).astype(BF16)
    gt_ex, _, _ = prefix(gt16)
    eq_ex, _, _ = prefix(eq16)
    sel = gt | (eq & (eq_ex < need))
    pos = gt_ex + jnp.minimum(eq_ex, need)
    q_ref[0] = jnp.where(sel, pos, -1.0).astype(I32)

    sel16 = jnp.where(sel, 1.0, 0.0).astype(BF16)
    _, cnt_c, bex_c = prefix(sel16)
    bin_c = bex_c + cnt_c
    lower_incl = (c128 <= r128).astype(BF16)
    incl_t = lax.dot_general(lower_incl, sel16, (((1,), (1,)), ((), ())), preferred_element_type=F32)
    cnt_r = lax.dot_general(jnp.ones((8, LANE), BF16), sel16, (((1,), (1,)), ((), ())),
                            preferred_element_type=F32)
    upper_strict = (rb < cb).astype(BF16)
    bex_r = jnp.dot(cnt_r.astype(BF16), upper_strict, preferred_element_type=F32)
    starts_ref[0] = bex_r[0:1, :].astype(I32)
    g_t = incl_t + bex_r[0:1, :]
    g_hi = jnp.floor(g_t * (1.0 / 64.0))
    g_lo = g_t - 64.0 * g_hi
    g_hi16 = g_hi.astype(BF16)
    g_lo16 = g_lo.astype(BF16)
    reps = jc // LANE
    bex_w = jnp.concatenate([bex_c] * reps, axis=1)
    bin_w = jnp.concatenate([bin_c] * reps, axis=1)
    lane_nb = lax.broadcasted_iota(I32, (nb, jc), 1).astype(F32)
    lane_128 = lax.broadcasted_iota(I32, (LANE, jc), 1).astype(F32)

    def chunk(c, carry):
        base = jnp.asarray(c * jc, I32).astype(F32)
        j_nb = lane_nb + base
        onehot_t = jnp.where((bex_w <= j_nb) & (j_nb < bin_w), 1.0, 0.0).astype(BF16)
        rows = (64.0 * jnp.dot(g_hi16, onehot_t, preferred_element_type=F32)
                + jnp.dot(g_lo16, onehot_t, preferred_element_type=F32))
        offs = jnp.sum(jnp.where(rows <= lane_128 + base, 1.0, 0.0), axis=0, keepdims=True)
        blk = jnp.sum(jnp.where(bin_w <= j_nb, 1.0, 0.0), axis=0, keepdims=True)
        idx_ref[0, :, pl.ds(pl.multiple_of(c * jc, jc), jc)] = (blk * float(LANE) + offs).astype(I32)
        return carry

    lax.fori_loop(0, cap // jc, chunk, 0)


def _route(aff_t, cap):
    e, t = aff_t.shape
    nb = t // LANE
    jc = min(256, cap)
    return pl.pallas_call(
        functools.partial(_route_kernel, cap=cap, jc=jc),
        grid=(e,),
        in_specs=[pl.BlockSpec((1, nb, LANE), lambda i: (i, 0, 0))],
        out_specs=[
            pl.BlockSpec((1, nb, LANE), lambda i: (i, 0, 0)),
            pl.BlockSpec((1, 1, cap), lambda i: (i, 0, 0)),
            pl.BlockSpec((1, 1, nb), lambda i: (i, 0, 0)),
        ],
        out_shape=[
            jax.ShapeDtypeStruct((e, nb, LANE), I32),
            jax.ShapeDtypeStruct((e, 1, cap), I32),
            jax.ShapeDtypeStruct((e, 1, nb), I32),
        ],
        compiler_params=_cparams(("arbitrary",)),
        name="route",
    )(aff_t.reshape(e, nb, LANE))


def _expert_kernel(idx_hbm, h_hbm, wg_ref, wu_ref, wd_ref, ye_ref, idx_smem, xbuf, x16, sem_idx, sem_rows, fence,
                   *, rm, nj, nsteps, fc):
    step = pl.program_id(0) * nj + pl.program_id(1)
    slot = step % 2

    def idx_copy(st):
        return pltpu.make_async_copy(idx_hbm.at[st % nsteps], idx_smem.at[st % 3], sem_idx.at[st % 3])

    def row_copy(st, sl, r):
        tok = idx_smem[st % 3, 0, r]
        return pltpu.make_async_copy(h_hbm.at[pl.ds(tok, 1)], xbuf.at[sl, pl.ds(r, 1)], sem_rows.at[sl])

    def rows_wait(sl):
        pltpu.make_async_copy(h_hbm.at[pl.ds(0, rm)], xbuf.at[sl], sem_rows.at[sl]).wait()

    @pl.when(step == 0)
    def _():
        first = idx_copy(0)
        first.start()
        first.wait()
        idx_copy(1).start()

        def issue(r, carry):
            row_copy(0, 0, r).start()
            return carry

        lax.fori_loop(0, rm, issue, 0)

    idx_copy(step + 1).wait()
    idx_copy(step + 2).start()
    rows_wait(slot)

    x16[...] = xbuf[slot].astype(BF16)
    nchunks = D_EXPERT // fc
    per = -(-rm // (nchunks - 1))
    acc = jnp.zeros((rm, D_MODEL), F32)
    for c in range(nchunks):
        for r in range(c * per, min(rm, (c + 1) * per)):
            row_copy(step + 1, 1 - slot, r).start()
        sl = slice(c * fc, (c + 1) * fc)
        g = jnp.dot(x16[...], wg_ref[0, :, sl], preferred_element_type=F32)
        u = jnp.dot(x16[...], wu_ref[0, :, sl], preferred_element_type=F32)
        hid = (_silu(g) * u).astype(BF16)
        acc = acc + jnp.dot(hid, wd_ref[0, sl, :], preferred_element_type=F32)
        pl.semaphore_signal(fence, 1)
        pl.semaphore_wait(fence, 1)
    ye_ref[...] = acc.astype(BF16)

    @pl.when(step == nsteps - 1)
    def _():
        idx_copy(step + 2).wait()
        rows_wait(1 - slot)


def _experts(idx, h2, wg16, wu16, wd16, rm):
    e, _, cap = idx.shape
    nj = cap // rm
    t, d = h2.shape
    idx_steps = idx.reshape(e * nj, 1, rm)
    return pl.pallas_call(
        functools.partial(_expert_kernel, rm=rm, nj=nj, nsteps=e * nj, fc=512),
        grid=(e, nj),
        in_specs=[
            pl.BlockSpec(memory_space=pl.ANY),
            pl.BlockSpec(memory_space=pl.ANY),
            pl.BlockSpec((1, d, D_EXPERT), lambda i, j: (i, 0, 0)),
            pl.BlockSpec((1, d, D_EXPERT), lambda i, j: (i, 0, 0)),
            pl.BlockSpec((1, D_EXPERT, d), lambda i, j: (i, 0, 0)),
        ],
        out_specs=pl.BlockSpec((rm, d), lambda i, j: (i * nj + j, 0)),
        out_shape=jax.ShapeDtypeStruct((e * cap, d), BF16),
        scratch_shapes=[
            pltpu.SMEM((3, 1, rm), I32),
            pltpu.VMEM((2, rm, d), F32),
            pltpu.VMEM((rm, d), BF16),
            pltpu.SemaphoreType.DMA((3,)),
            pltpu.SemaphoreType.DMA((2,)),
            pltpu.SemaphoreType.REGULAR,
        ],
        compiler_params=_cparams(("arbitrary", "arbitrary")),
        name="expert",
    )(idx_steps, h2, wg16, wu16, wd16)


def _combine_kernel(starts_ref, q_ref, aff_ref, x1_ref, g2_ref, fnw_ref, ye_hbm, y_ref, ybuf, sem,
                    *, cap, win, short, tm):
    i = pl.program_id(0)
    n = pl.num_programs(0)
    slot = i % 2
    total = N_EXPERTS * cap

    def plan(tile):
        starts, all_short = [], None
        for e in range(N_EXPERTS):
            s0 = starts_ref[tile * N_EXPERTS + e]
            s1 = starts_ref[(tile + 1) * N_EXPERTS + e]
            gs = e * cap + s0
            a = pl.multiple_of(jnp.minimum((gs // BF16_ROWS) * BF16_ROWS, total - win), BF16_ROWS)
            fits = gs + (s1 - s0) - a <= short
            all_short = fits if all_short is None else jnp.logical_and(all_short, fits)
            starts.append(a)
        return starts, all_short

    def for_copies(tile, sl, act):
        starts, all_short = plan(tile)

        def run(rows):
            for e in range(N_EXPERTS):
                act(pltpu.make_async_copy(ye_hbm.at[pl.ds(starts[e], rows)], ybuf.at[sl, pl.ds(e * win, rows)],
                                          sem.at[sl]))

        pl.when(all_short)(lambda: run(short))
        pl.when(jnp.logical_not(all_short))(lambda: run(win))

    @pl.when(i == 0)
    def _():
        ybuf[...] = jnp.zeros_like(ybuf)
        for_copies(0, 0, lambda cp: cp.start())

    @pl.when(i + 1 < n)
    def _():
        for_copies(i + 1, 1 - slot, lambda cp: cp.start())

    for_copies(i, slot, lambda cp: cp.wait())

    starts, _ = plan(i)
    q = q_ref[...]
    aff = aff_ref[...]
    expert = lax.broadcasted_iota(I32, (N_EXPERTS, tm), 0)
    shift = jnp.zeros((N_EXPERTS, tm), I32)
    for e in range(N_EXPERTS):
        shift = jnp.where(expert == e, e * cap - starts[e], shift)
    row = jnp.where(q >= 0, q + shift, jnp.int32(-1))
    win_row = lax.broadcasted_iota(I32, (win, tm), 0)
    val_t = jnp.concatenate(
        [jnp.where(row[e:e + 1, :] == win_row, aff[e:e + 1, :], 0.0)
         for e in range(N_EXPERTS)], axis=0)
    val = val_t.T
    hi = val.astype(BF16)
    lo = (val - hi.astype(F32)).astype(BF16)
    both = jnp.dot(jnp.concatenate([hi, lo], axis=0), ybuf[slot], preferred_element_type=F32)
    out = x1_ref[...] + g2_ref[0] * (both[:tm] + both[tm:])
    y_ref[...] = _rms(out) * fnw_ref[...]


def _combine(starts_flat, q_slot, aff_t, x1, g2, final_norm_w, ye, cap, tm, tiles_per_batch):
    t, d = x1.shape
    win = tm + BF16_ROWS
    short = 4 * BF16_ROWS
    grid_spec = pltpu.PrefetchScalarGridSpec(
        num_scalar_prefetch=1,
        grid=(t // tm,),
        in_specs=[
            pl.BlockSpec((N_EXPERTS, tm), lambda i, s: (0, i)),
            pl.BlockSpec((N_EXPERTS, tm), lambda i, s: (0, i)),
            pl.BlockSpec((tm, d), lambda i, s: (i, 0)),
            pl.BlockSpec((1, 1, d), lambda i, s: (i // tiles_per_batch, 0, 0)),
            pl.BlockSpec((1, d), lambda i, s: (0, 0)),
            pl.BlockSpec(memory_space=pl.ANY),
        ],
        out_specs=pl.BlockSpec((tm, d), lambda i, s: (i, 0)),
        scratch_shapes=[
            pltpu.VMEM((2, N_EXPERTS * win, d), BF16),
            pltpu.SemaphoreType.DMA((2,)),
        ],
    )
    return pl.pallas_call(
        functools.partial(_combine_kernel, cap=cap, win=win, short=short, tm=tm),
        grid_spec=grid_spec,
        out_shape=jax.ShapeDtypeStruct((t, d), F32),
        compiler_params=_cparams(("arbitrary",)),
        name="combine",
    )(starts_flat, q_slot, aff_t, x1, g2, final_norm_w.reshape(1, d), ye)


def _rotate_half_cols(w):
    half = ROPE // 2
    return jnp.concatenate([-w[..., half:], w[..., :half]], axis=-1)


def _prep_weights(w_in, w_uq, w_ukv, w_out, w_gate, w_up, w_down):
    d = w_in.shape[0]
    offs = [Q_RANK, Q_RANK + KV_RANK, Q_RANK + KV_RANK + ROPE, Q_RANK + KV_RANK + ROPE + D_SSD,
            Q_RANK + KV_RANK + ROPE + D_SSD + CONV_DIM]
    w_cq, w_ckv, w_kr, w_z, w_xbc, w_dt = jnp.split(w_in, offs, axis=1)
    place = lambda w: jnp.pad(w, ((0, 0), (NOPE, HEAD_PAD - NOPE - ROPE)))
    win_ext = jnp.concatenate(
        [w_cq, w_ckv, place(w_kr), place(_rotate_half_cols(w_kr)), w_z, w_xbc,
         jnp.pad(w_dt, ((0, 0), (0, LANE - w_dt.shape[1])))], axis=1).astype(BF16)
    assert win_ext.shape == (d, _N_IN)

    wq = w_uq.reshape(Q_RANK, N_HEADS, NOPE + ROPE)
    pad_h = lambda w: jnp.pad(w, ((0, 0), (0, 0), (0, HEAD_PAD - NOPE - ROPE))).reshape(Q_RANK, N_HEADS * HEAD_PAD)
    wq_rot = jnp.concatenate([jnp.zeros_like(wq[..., :NOPE]), _rotate_half_cols(wq[..., NOPE:])], axis=-1)
    wq_ext = jnp.concatenate([pad_h(wq), pad_h(wq_rot)], axis=1).astype(BF16)

    wkv = w_ukv.reshape(KV_RANK, N_HEADS, NOPE + V_DIM)
    wk = jnp.pad(wkv[..., :NOPE], ((0, 0), (0, 0), (0, HEAD_PAD - NOPE))).reshape(KV_RANK, N_HEADS * HEAD_PAD)
    wv = jnp.pad(wkv[..., NOPE:], ((0, 0), (0, 0), (0, HEAD_PAD - V_DIM))).reshape(KV_RANK, N_HEADS * HEAD_PAD)
    wkv_ext = jnp.concatenate([wk, wv], axis=1).astype(BF16)
    return win_ext, wq_ext, wkv_ext, w_out.astype(BF16), w_gate.astype(BF16), w_up.astype(BF16), w_down.astype(BF16)


def _rope_table(seq):
    pos = jnp.arange(seq, dtype=F32)
    inv = 1.0 / (ROPE_THETA ** (jnp.arange(0, ROPE, 2, dtype=F32) / ROPE))
    ang = pos[:, None] * inv[None, :]
    cos = jnp.concatenate([jnp.cos(ang)] * 2, axis=-1)
    sin = jnp.concatenate([jnp.sin(ang)] * 2, axis=-1)
    zpad = jnp.zeros((seq, HEAD_PAD - NOPE - ROPE), F32)
    qs = (NOPE + ROPE) ** -0.5 * math.log2(math.e)
    cosq = jnp.concatenate([jnp.ones((seq, NOPE), F32), cos, zpad], axis=-1) * qs
    sinq = jnp.concatenate([jnp.zeros((seq, NOPE), F32), sin, zpad], axis=-1) * qs
    cosk = jnp.concatenate([jnp.zeros((seq, NOPE), F32), cos, zpad], axis=-1)
    sink = jnp.concatenate([jnp.zeros((seq, NOPE), F32), sin, zpad], axis=-1)
    return jnp.concatenate([cosq, sinq, cosk, sink], axis=-1)


def _conv_halo(xbc, tile):
    b, s, c = xbc.shape
    nc = s // tile
    r = xbc.reshape(b, nc, tile, c)
    z2 = jnp.zeros((b, 1, 2, c), xbc.dtype)
    prev = jnp.concatenate([z2, r[:, :-1, tile - 2:, :]], axis=1)
    nxt = jnp.concatenate([r[:, 1:, :2, :], z2], axis=1)
    return jnp.concatenate([prev, nxt, jnp.zeros((b, nc, 4, c), xbc.dtype)], axis=2)


def _trunk(x, mod, p):
    b, s, d = x.shape
    t = b * s
    cap = max(1, CAP_FACTOR * t // N_EXPERTS)
    sh1, sc1, g1, sh2, sc2, g2 = [m.reshape(b, 1, d) for m in jnp.split(mod, 6, axis=-1)]

    tm = min(512, s)
    q, k, v, z, xbc, dt = _inproj(x, sc1, sh1, p["norm1_w"], p["win_ext"], p["q_norm_w"], p["wq_ext"],
                                  p["kv_norm_w"], p["wkv_ext"], _rope_table(s), tm)
    ya = _attention(q, k, v, min(s, max(LANE, _SCORE_ELEMS // s)))
    yf, yb = _ssd(xbc, dt, p["conv_w8"], p["conv_b"], p["alog_flat"], p["dtb_flat"], p["dsk_flat"],
                  2 if b % 2 == 0 else 1)
    x1, h2, aff_t = _outproj(ya, yf, yb, z, x, g1, sc2, sh2, p["attn_norm_w"], p["ssd_norm_w"], p["wout16"],
                             p["norm2_w"], p["w_router"], tm)
    q_slot, idx, starts = _route(aff_t, cap)
    ye = _experts(idx, h2.reshape(t, d), p["wg16"], p["wu16"], p["wd16"], min(512, cap))
    starts_flat = jnp.concatenate([starts.reshape(N_EXPERTS, t // LANE).T,
                                   jnp.full((1, N_EXPERTS), cap, I32)], axis=0).reshape(-1)
    y = _combine(starts_flat, q_slot.reshape(N_EXPERTS, t), aff_t, x1.reshape(t, d), g2, p["final_norm_w"], ye,
                 cap, LANE, s // LANE)
    return y.reshape(b, s, d)


def kernel(x_prompt, x_sample, c_prompt, c_sample, w_ada, b_ada, norm1_w, w_in, q_norm_w, w_uq, kv_norm_w, w_ukv,
           attn_norm_w, conv_w, conv_b, a_log, dt_bias, d_skip, ssd_norm_w, w_out, norm2_w, w_router, w_gate, w_up,
           w_down, final_norm_w):
    win_ext, wq_ext, wkv_ext, wout16, wg16, wu16, wd16 = _prep_weights(
        w_in[0], w_uq[0], w_ukv[0], w_out[0], w_gate[0], w_up[0], w_down[0])
    flat16 = lambda a: jnp.pad(a.reshape(1, 2 * SSD_HEADS), ((0, 0), (0, LANE - 2 * SSD_HEADS)))
    p = dict(
        norm1_w=norm1_w[0], win_ext=win_ext, q_norm_w=q_norm_w[0], wq_ext=wq_ext, kv_norm_w=kv_norm_w[0],
        wkv_ext=wkv_ext, attn_norm_w=attn_norm_w[0], ssd_norm_w=ssd_norm_w[0], wout16=wout16, norm2_w=norm2_w[0],
        w_router=w_router[0], wg16=wg16, wu16=wu16, wd16=wd16, final_norm_w=final_norm_w,
        conv_w8=jnp.pad(conv_w[0], ((0, 8 - D_CONV), (0, 0))), conv_b=conv_b[0].reshape(1, CONV_DIM),
        alog_flat=flat16(a_log[0]), dtb_flat=flat16(dt_bias[0]),
        dsk_flat=jnp.repeat(d_skip[0], SSD_P).reshape(1, D_SSD),
    )
    nbp = c_prompt.shape[0]
    mod = _mod(jnp.concatenate([c_prompt, c_sample], axis=0), w_ada[0], b_ada[0])
    return _trunk(x_prompt, mod[:nbp], p), _trunk(x_sample, mod[nbp:], p)
```

```python
import functools
import math

import jax
import jax.numpy as jnp
from jax import lax
from jax.experimental import pallas as pl
from jax.experimental.pallas import tpu as pltpu

F32 = jnp.float32
BF16 = jnp.bfloat16
I32 = jnp.int32
HIGHEST = lax.Precision.HIGHEST

D_MODEL = 1024
N_HEADS = 8
NOPE = 64
ROPE = 32
V_DIM = 64
Q_RANK = 256
KV_RANK = 128
ROPE_THETA = 10000.0
D_ATTN = N_HEADS * V_DIM
SSD_HEADS = 8
SSD_P = 64
D_SSD = SSD_HEADS * SSD_P
SSD_G = 2
SSD_N = 64
D_CONV = 5
CHUNK = 128
CONV_DIM = D_SSD + 2 * SSD_G * SSD_N
N_EXPERTS = 16
CAP_FACTOR = 2
D_EXPERT = 2048
EPS = 1e-6

LANE = 128
HEAD_PAD = 128
BF16_ROWS = 16
VMEM_LIMIT = 56 * 1024 * 1024
_SCORE_ELEMS = 1 << 20

_C_CQ = 0
_C_CKV = _C_CQ + Q_RANK
_C_KR = _C_CKV + KV_RANK
_C_KRR = _C_KR + LANE
_C_Z = _C_KRR + LANE
_C_XBC = _C_Z + D_SSD
_C_DT = _C_XBC + CONV_DIM
_N_IN = _C_DT + LANE


def _cparams(sem, vmem=VMEM_LIMIT):
    return pltpu.CompilerParams(dimension_semantics=sem, vmem_limit_bytes=vmem)


def _silu(x):
    return x / (1.0 + jnp.exp(-x))


def _rms(x):
    return x * lax.rsqrt(jnp.mean(x * x, axis=-1, keepdims=True) + EPS)


def _mod_kernel(c_ref, w_ref, b_ref, o_ref):
    a = _silu(c_ref[...])
    o_ref[...] = jnp.dot(a, w_ref[...], precision=HIGHEST, preferred_element_type=F32) + b_ref[...]


def _mod(c, w_ada, b_ada):
    nb, d = c.shape
    n = w_ada.shape[1]
    tn = 1536
    return pl.pallas_call(
        _mod_kernel,
        grid=(n // tn,),
        in_specs=[
            pl.BlockSpec((nb, d), lambda j: (0, 0)),
            pl.BlockSpec((d, tn), lambda j: (0, j)),
            pl.BlockSpec((1, tn), lambda j: (0, j)),
        ],
        out_specs=pl.BlockSpec((nb, tn), lambda j: (0, j)),
        out_shape=jax.ShapeDtypeStruct((nb, n), F32),
        compiler_params=_cparams(("arbitrary",)),
        name="mod",
    )(c, w_ada, b_ada.reshape(1, n))


def _inproj_kernel(x_ref, sc_ref, sh_ref, n1_ref, win_ref, qn_ref, wq_ref, kvn_ref, wkv_ref, tab_ref,
                   q_ref, k_ref, v_ref, z_ref, xbc_ref, dt_ref):
    h = _rms(x_ref[0]) * n1_ref[...]
    h = h * (1.0 + sc_ref[0]) + sh_ref[0]
    proj = jnp.dot(h.astype(BF16), win_ref[...], preferred_element_type=F32)
    z_ref[0] = proj[:, _C_Z:_C_XBC]
    xbc_ref[0] = proj[:, _C_XBC:_C_DT]
    dt_ref[0] = proj[:, _C_DT:_N_IN]

    tab = tab_ref[...]
    cosq, sinq = tab[:, 0:LANE], tab[:, LANE:2 * LANE]
    cosk, sink = tab[:, 2 * LANE:3 * LANE], tab[:, 3 * LANE:4 * LANE]

    cqn = (_rms(proj[:, _C_CQ:_C_CKV]) * qn_ref[...]).astype(BF16)
    q2 = jnp.dot(cqn, wq_ref[...], preferred_element_type=F32)
    ckvn = (_rms(proj[:, _C_CKV:_C_KR]) * kvn_ref[...]).astype(BF16)
    kv = jnp.dot(ckvn, wkv_ref[...], preferred_element_type=F32)
    krope = proj[:, _C_KR:_C_KRR] * cosk + proj[:, _C_KRR:_C_Z] * sink
    nq = N_HEADS * HEAD_PAD
    is_v = lax.broadcasted_iota(I32, (x_ref.shape[1], HEAD_PAD), 1) < V_DIM
    for hd in range(N_HEADS):
        sl = slice(hd * HEAD_PAD, (hd + 1) * HEAD_PAD)
        rot = slice(nq + hd * HEAD_PAD, nq + (hd + 1) * HEAD_PAD)
        q_ref[0, hd] = (q2[:, sl] * cosq + q2[:, rot] * sinq).astype(BF16)
        k_ref[0, hd] = (kv[:, sl] + krope).astype(BF16)
        v_ref[0, hd] = jnp.where(is_v, kv[:, rot], 1.0).astype(BF16)


def _inproj(x, sc1, sh1, norm1_w, win_ext, q_norm_w, wq_ext, kv_norm_w, wkv_ext, tab, tm):
    b, s, d = x.shape
    vec = lambda n: pl.BlockSpec((1, n), lambda i, j: (0, 0))
    full = lambda a: pl.BlockSpec(a.shape, lambda i, j: (0, 0))
    per_b = pl.BlockSpec((1, 1, d), lambda i, j: (i, 0, 0))
    tok = lambda n: pl.BlockSpec((1, tm, n), lambda i, j: (i, j, 0))
    heads = pl.BlockSpec((1, N_HEADS, tm, HEAD_PAD), lambda i, j: (i, 0, j, 0))
    head_shape = jax.ShapeDtypeStruct((b, N_HEADS, s, HEAD_PAD), BF16)
    return pl.pallas_call(
        _inproj_kernel,
        grid=(b, s // tm),
        in_specs=[tok(d), per_b, per_b, vec(d), full(win_ext), vec(Q_RANK), full(wq_ext), vec(KV_RANK),
                  full(wkv_ext), pl.BlockSpec((tm, 4 * LANE), lambda i, j: (j, 0))],
        out_specs=[heads, heads, heads, tok(D_SSD), tok(CONV_DIM), tok(LANE)],
        out_shape=[
            head_shape, head_shape, head_shape,
            jax.ShapeDtypeStruct((b, s, D_SSD), F32),
            jax.ShapeDtypeStruct((b, s, CONV_DIM), F32),
            jax.ShapeDtypeStruct((b, s, LANE), F32),
        ],
        compiler_params=_cparams(("parallel", "arbitrary")),
        name="inproj",
    )(x, sc1, sh1, norm1_w.reshape(1, d), win_ext, q_norm_w.reshape(1, -1), wq_ext,
      kv_norm_w.reshape(1, -1), wkv_ext, tab)


def _attn_kernel(q_ref, k_ref, v_ref, o_ref, s_scr):
    lane = lax.broadcasted_iota(I32, (q_ref.shape[2], HEAD_PAD), 1)
    first = lane < V_DIM

    def scores(h):
        s = lax.dot_general(q_ref[0, h], k_ref[0, h], (((1,), (1,)), ((), ())), preferred_element_type=F32)
        s_scr[h % 2] = s
        return jnp.max(s, axis=-1, keepdims=True)

    def finish(h, m):
        p = jnp.exp2(s_scr[h % 2] - m).astype(BF16)
        o = jnp.dot(p, v_ref[0, h], preferred_element_type=F32)
        return o / pltpu.roll(o, V_DIM, axis=1)

    def emit(h, o, prev):
        if h % 2 == 0:
            return o
        pair = h // 2
        o_ref[0, :, pair * HEAD_PAD:(pair + 1) * HEAD_PAD] = jnp.where(first, prev, pltpu.roll(o, V_DIM, axis=1))
        return None

    m_prev = scores(0)
    held = None
    for h in range(1, N_HEADS):
        m = scores(h)
        held = emit(h - 1, finish(h - 1, m_prev), held)
        m_prev = m
    emit(N_HEADS - 1, finish(N_HEADS - 1, m_prev), held)


def _attention(q, k, v, tq):
    b, _, s, _ = q.shape
    kv_spec = pl.BlockSpec((1, N_HEADS, s, HEAD_PAD), lambda i, j: (i, 0, 0, 0))
    return pl.pallas_call(
        _attn_kernel,
        grid=(b, s // tq),
        in_specs=[pl.BlockSpec((1, N_HEADS, tq, HEAD_PAD), lambda i, j: (i, 0, j, 0)), kv_spec, kv_spec],
        out_specs=pl.BlockSpec((1, tq, D_ATTN), lambda i, j: (i, j, 0)),
        out_shape=jax.ShapeDtypeStruct((b, s, D_ATTN), F32),
        scratch_shapes=[pltpu.VMEM((2, tq, s), F32)],
        compiler_params=_cparams(("parallel", "arbitrary")),
        name="attn",
    )(q, k, v)


def _ssd_direction(xbc_ref, halo_ref, dt_ref, cw_ref, cb_ref, alog_ref, dtb_ref, dsk_ref, ext_ref, st_ref, y_ref,
                   row, reverse):
    q = CHUNK
    ext = ext_ref.at[2 * row + int(reverse)]
    ext[6:8, :] = halo_ref[row, 0, 0:2, :]
    ext[8:8 + q, :] = xbc_ref[row]
    ext[8 + q:10 + q, :] = halo_ref[row, 0, 2:4, :]
    conv = cb_ref[...] + cw_ref[0:1, :] * ext[6:6 + q, :]
    for kk in range(1, D_CONV):
        conv = conv + cw_ref[kk:kk + 1, :] * ext[6 + kk:6 + kk + q, :]
    act = _silu(conv)
    bm = act[:, D_SSD:D_SSD + LANE]
    cm = act[:, D_SSD + LANE:D_SSD + 2 * LANE]

    off = SSD_HEADS if reverse else 0
    raw = dt_ref[row] + dtb_ref[...]
    dt = jnp.maximum(raw, 0.0) + jnp.log1p(jnp.exp(-jnp.abs(raw)))
    dta = dt * (-jnp.exp(alog_ref[...]))
    ri = lax.broadcasted_iota(I32, (q, q), 0)
    ci = lax.broadcasted_iota(I32, (q, q), 1)
    mask = (ci >= ri) if reverse else (ci <= ri)
    cs = jnp.dot(mask.astype(F32), dta, precision=HIGHEST, preferred_element_type=F32)
    cs_t = cs.T
    dt_t = dt.T
    end = 0 if reverse else q - 1
    cs_end = cs[end:end + 1, :]
    wdec = jnp.exp(cs_end - cs) * dt
    ecs = jnp.exp(cs)
    cdec = jnp.exp(cs_end)

    lane = lax.broadcasted_iota(I32, (q, LANE), 1)
    sub = lax.broadcasted_iota(I32, (LANE, LANE), 0)
    cm16 = cm.astype(BF16)
    cbs = []
    for g in range(SSD_G):
        in_g = (lane >= g * SSD_N) & (lane < (g + 1) * SSD_N)
        cg = jnp.where(in_g, cm, 0.0).astype(BF16)
        cbs.append(lax.dot_general(cg, bm.astype(BF16), (((1,), (1,)), ((), ())), preferred_element_type=F32))

    for j in range(SSD_HEADS // 2):
        g = (2 * j) // (SSD_HEADS // SSD_G)
        xp = act[:, j * LANE:(j + 1) * LANE]
        xp16 = xp.astype(BF16)
        yd = []
        for hh in (2 * j, 2 * j + 1):
            ln = off + hh
            seg = cs[:, ln:ln + 1] - cs_t[ln:ln + 1, :]
            dec = jnp.exp(jnp.where(mask, seg, -jnp.inf))
            m = (cbs[g] * dec * dt_t[ln:ln + 1, :]).astype(BF16)
            yd.append(jnp.dot(m, xp16, preferred_element_type=F32))
        la, lb = off + 2 * j, off + 2 * j + 1
        first = lane < SSD_P
        y = jnp.where(first, yd[0], yd[1])
        st = st_ref[row, j]
        y_off = lax.dot_general(cm16, st.astype(BF16), (((1,), (1,)), ((), ())), preferred_element_type=F32)
        y = y + y_off * jnp.where(first, ecs[:, la:la + 1], ecs[:, lb:lb + 1])
        xw = (xp * jnp.where(first, wdec[:, la:la + 1], wdec[:, lb:lb + 1])).astype(BF16)
        in_g = (lane >= g * SSD_N) & (lane < (g + 1) * SSD_N)
        bg = jnp.where(in_g, bm, 0.0).astype(BF16)
        new = lax.dot_general(xw, bg, (((0,), (0,)), ((), ())), preferred_element_type=F32)
        st_ref[row, j] = st * jnp.where(sub < SSD_P, cdec[:, la:la + 1], cdec[:, lb:lb + 1]) + new
        if not reverse:
            y = y + dsk_ref[:, j * LANE:(j + 1) * LANE] * xp
        y_ref[row, :, j * LANE:(j + 1) * LANE] = y


def _ssd_kernel(xf_ref, hf_ref, dtf_ref, xb_ref, hb_ref, dtb_in_ref, cw_ref, cb_ref, alog_ref, dtbias_ref, dsk_ref,
                yf_ref, yb_ref, ext_ref, stf_ref, stb_ref):
    @pl.when(pl.program_id(1) == 0)
    def _():
        stf_ref[...] = jnp.zeros_like(stf_ref)
        stb_ref[...] = jnp.zeros_like(stb_ref)

    for row in range(xf_ref.shape[0]):
        _ssd_direction(xf_ref, hf_ref, dtf_ref, cw_ref, cb_ref, alog_ref, dtbias_ref, dsk_ref, ext_ref, stf_ref,
                       yf_ref, row, reverse=False)
        _ssd_direction(xb_ref, hb_ref, dtb_in_ref, cw_ref, cb_ref, alog_ref, dtbias_ref, dsk_ref, ext_ref, stb_ref,
                       yb_ref, row, reverse=True)


def _ssd(xbc, dt, conv_w8, conv_b, alog_flat, dtb_flat, dsk_flat, rows):
    b, s, _ = xbc.shape
    nc = s // CHUNK
    halo = _conv_halo(xbc, CHUNK)
    fwd = lambda n: pl.BlockSpec((rows, CHUNK, n), lambda i, c: (i, c, 0))
    bwd = lambda n: pl.BlockSpec((rows, CHUNK, n), lambda i, c: (i, nc - 1 - c, 0))
    hf = pl.BlockSpec((rows, 1, 8, CONV_DIM), lambda i, c: (i, c, 0, 0))
    hb = pl.BlockSpec((rows, 1, 8, CONV_DIM), lambda i, c: (i, nc - 1 - c, 0, 0))
    const = lambda a: pl.BlockSpec(a.shape, lambda i, c: (0, 0))
    return pl.pallas_call(
        _ssd_kernel,
        grid=(b // rows, nc),
        in_specs=[fwd(CONV_DIM), hf, fwd(LANE), bwd(CONV_DIM), hb, bwd(LANE),
                  const(conv_w8), const(conv_b), const(alog_flat), const(dtb_flat), const(dsk_flat)],
        out_specs=[fwd(D_SSD), bwd(D_SSD)],
        out_shape=[jax.ShapeDtypeStruct((b, s, D_SSD), F32)] * 2,
        scratch_shapes=[
            pltpu.VMEM((2 * rows, CHUNK + 16, CONV_DIM), F32),
            pltpu.VMEM((rows, SSD_HEADS // 2, LANE, LANE), F32),
            pltpu.VMEM((rows, SSD_HEADS // 2, LANE, LANE), F32),
        ],
        compiler_params=_cparams(("parallel", "arbitrary")),
        name="ssd",
    )(xbc, halo, dt, xbc, halo, dt, conv_w8, conv_b, alog_flat, dtb_flat, dsk_flat)


def _outproj_kernel(ya_ref, yf_ref, yb_ref, z_ref, x_ref, g1_ref, sc2_ref, sh2_ref, anw_ref, snw_ref, wout_ref,
                    n2_ref, wrt_ref, x1_ref, h2_ref, afft_ref):
    ya = _rms(ya_ref[0]) * anw_ref[...]
    y = (yf_ref[0] + yb_ref[0]) * _silu(z_ref[0])
    half = D_SSD // SSD_G
    snw = snw_ref[...]
    parts = [ya]
    for g in range(SSD_G):
        sl = slice(g * half, (g + 1) * half)
        parts.append(_rms(y[:, sl]) * snw[:, sl])
    ycat = jnp.concatenate(parts, axis=-1).astype(BF16)
    o = jnp.dot(ycat, wout_ref[...], preferred_element_type=F32)
    x1 = x_ref[0] + g1_ref[0] * o
    x1_ref[0] = x1
    h2 = _rms(x1) * n2_ref[...]
    h2 = h2 * (1.0 + sc2_ref[0]) + sh2_ref[0]
    h2_ref[0] = h2

    lt = lax.dot_general(wrt_ref[...], h2, (((1,), (1,)), ((), ())), precision=HIGHEST,
                         preferred_element_type=F32)
    et = jnp.exp(lt - jnp.max(lt, axis=0, keepdims=True))
    afft_ref[...] = et / jnp.sum(et, axis=0, keepdims=True)


def _outproj(ya, yf, yb, z, x, g1, sc2, sh2, attn_norm_w, ssd_norm_w, wout16, norm2_w, w_router, tm):
    b, s, d = x.shape
    t = b * s
    nt = s // tm
    tok = lambda n: pl.BlockSpec((1, tm, n), lambda i, j: (i, j, 0))
    per_b = pl.BlockSpec((1, 1, d), lambda i, j: (i, 0, 0))
    vec = lambda n: pl.BlockSpec((1, n), lambda i, j: (0, 0))
    full = lambda a: pl.BlockSpec(a.shape, lambda i, j: (0, 0))
    wrt = w_router.T
    return pl.pallas_call(
        _outproj_kernel,
        grid=(b, nt),
        in_specs=[tok(D_ATTN), tok(D_SSD), tok(D_SSD), tok(D_SSD), tok(d), per_b, per_b, per_b,
                  vec(D_ATTN), vec(D_SSD), full(wout16), vec(d), full(wrt)],
        out_specs=[tok(d), tok(d), pl.BlockSpec((N_EXPERTS, tm), lambda i, j: (0, i * nt + j))],
        out_shape=[
            jax.ShapeDtypeStruct((b, s, d), F32),
            jax.ShapeDtypeStruct((b, s, d), F32),
            jax.ShapeDtypeStruct((N_EXPERTS, t), F32),
        ],
        compiler_params=_cparams(("parallel", "arbitrary")),
        name="outproj",
    )(ya, yf, yb, z, x, g1, sc2, sh2, attn_norm_w.reshape(1, -1), ssd_norm_w.reshape(1, -1), wout16,
      norm2_w.reshape(1, d), wrt)


def _count(cond):
    c = jnp.sum(jnp.where(cond, 1.0, 0.0), axis=0, keepdims=True)
    return jnp.sum(c, axis=1, keepdims=True)


def _route_kernel(aff_ref, q_ref, idx_ref, starts_ref, *, cap, jc):
    a = aff_ref[0]
    nb = a.shape[0]
    keys = pltpu.bitcast(a, I32)
    kf = jnp.float32(cap)

    tau = jnp.zeros((1, 1), I32)
    for bit in range(30, -1, -1):
        cand = tau | jnp.int32(1 << bit)
        tau = jnp.where(_count(keys >= cand) >= kf, cand, tau)
    gt = keys > tau
    eq = keys == tau
    need = kf - _count(gt)

    r128 = lax.broadcasted_iota(I32, (LANE, LANE), 0)
    c128 = lax.broadcasted_iota(I32, (LANE, LANE), 1)
    upper = (r128 <= c128).astype(BF16)
    ones = jnp.ones((LANE, LANE), BF16)
    rb = lax.broadcasted_iota(I32, (nb, nb), 0)
    cb = lax.broadcasted_iota(I32, (nb, nb), 1)
    lower_strict = (cb < rb).astype(BF16)

    def prefix(m16):
        incl = jnp.dot(m16, upper, preferred_element_type=F32)
        cnt = jnp.dot(m16, ones, preferred_element_type=F32)
        bex = jnp.dot(lower_strict, cnt.astype(BF16), preferred_element_type=F32)
        return bex + incl - m16.astype(F32), cnt, bex

    gt16 = jnp.where(gt, 1.0, 0.0).astype(BF16)
    eq16 = jnp.where(eq, 1.0, 0.0).astype(BF16)
    gt_ex, _, _ = prefix(gt16)
    eq_ex, _, _ = prefix(eq16)
    sel = gt | (eq & (eq_ex < need))
    pos = gt_ex + jnp.minimum(eq_ex, need)
    q_ref[0] = jnp.where(sel, pos, -1.0).astype(I32)

    sel16 = jnp.where(sel, 1.0, 0.0).astype(BF16)
    _, cnt_c, bex_c = prefix(sel16)
    bin_c = bex_c + cnt_c
    lower_incl = (c128 <= r128).astype(BF16)
    incl_t = lax.dot_general(lower_incl, sel16, (((1,), (1,)), ((), ())), preferred_element_type=F32)
    cnt_r = lax.dot_general(jnp.ones((8, LANE), BF16), sel16, (((1,), (1,)), ((), ())),
                            preferred_element_type=F32)
    upper_strict = (rb < cb).astype(BF16)
    bex_r = jnp.dot(cnt_r.astype(BF16), upper_strict, preferred_element_type=F32)
    starts_ref[0] = bex_r[0:1, :].astype(I32)
    g_t = incl_t + bex_r[0:1, :]
    g_hi = jnp.floor(g_t * (1.0 / 64.0))
    g_lo = g_t - 64.0 * g_hi
    g_hi16 = g_hi.astype(BF16)
    g_lo16 = g_lo.astype(BF16)
    reps = jc // LANE
    bex_w = jnp.concatenate([bex_c] * reps, axis=1)
    bin_w = jnp.concatenate([bin_c] * reps, axis=1)
    lane_nb = lax.broadcasted_iota(I32, (nb, jc), 1).astype(F32)
    lane_128 = lax.broadcasted_iota(I32, (LANE, jc), 1).astype(F32)

    def chunk(c, carry):
        base = jnp.asarray(c * jc, I32).astype(F32)
        j_nb = lane_nb + base
        onehot_t = jnp.where((bex_w <= j_nb) & (j_nb < bin_w), 1.0, 0.0).astype(BF16)
        rows = (64.0 * jnp.dot(g_hi16, onehot_t, preferred_element_type=F32)
                + jnp.dot(g_lo16, onehot_t, preferred_element_type=F32))
        offs = jnp.sum(jnp.where(rows <= lane_128 + base, 1.0, 0.0), axis=0, keepdims=True)
        blk = jnp.sum(jnp.where(bin_w <= j_nb, 1.0, 0.0), axis=0, keepdims=True)
        idx_ref[0, :, pl.ds(pl.multiple_of(c * jc, jc), jc)] = (blk * float(LANE) + offs).astype(I32)
        return carry

    lax.fori_loop(0, cap // jc, chunk, 0)


def _route(aff_t, cap):
    e, t = aff_t.shape
    nb = t // LANE
    jc = min(256, cap)
    return pl.pallas_call(
        functools.partial(_route_kernel, cap=cap, jc=jc),
        grid=(e,),
        in_specs=[pl.BlockSpec((1, nb, LANE), lambda i: (i, 0, 0))],
        out_specs=[
            pl.BlockSpec((1, nb, LANE), lambda i: (i, 0, 0)),
            pl.BlockSpec((1, 1, cap), lambda i: (i, 0, 0)),
            pl.BlockSpec((1, 1, nb), lambda i: (i, 0, 0)),
        ],
        out_shape=[
            jax.ShapeDtypeStruct((e, nb, LANE), I32),
            jax.ShapeDtypeStruct((e, 1, cap), I32),
            jax.ShapeDtypeStruct((e, 1, nb), I32),
        ],
        compiler_params=_cparams(("arbitrary",)),
        name="route",
    )(aff_t.reshape(e, nb, LANE))


def _expert_kernel(idx_hbm, h_hbm, wg_ref, wu_ref, wd_ref, ye_ref, idx_smem, xbuf, x16, sem_idx, sem_rows, fence,
                   *, rm, nj, nsteps, fc):
    step = pl.program_id(0) * nj + pl.program_id(1)
    slot = step % 2

    def idx_copy(st):
        return pltpu.make_async_copy(idx_hbm.at[st % nsteps], idx_smem.at[st % 3], sem_idx.at[st % 3])

    def row_copy(st, sl, r):
        tok = idx_smem[st % 3, 0, r]
        return pltpu.make_async_copy(h_hbm.at[pl.ds(tok, 1)], xbuf.at[sl, pl.ds(r, 1)], sem_rows.at[sl])

    def rows_wait(sl):
        pltpu.make_async_copy(h_hbm.at[pl.ds(0, rm)], xbuf.at[sl], sem_rows.at[sl]).wait()

    @pl.when(step == 0)
    def _():
        first = idx_copy(0)
        first.start()
        first.wait()
        idx_copy(1).start()

        def issue(r, carry):
            row_copy(0, 0, r).start()
            return carry

        lax.fori_loop(0, rm, issue, 0)

    idx_copy(step + 1).wait()
    idx_copy(step + 2).start()
    rows_wait(slot)

    x16[...] = xbuf[slot].astype(BF16)
    nchunks = D_EXPERT // fc
    per = -(-rm // (nchunks - 1))
    acc = jnp.zeros((rm, D_MODEL), F32)
    for c in range(nchunks):
        for r in range(c * per, min(rm, (c + 1) * per)):
            row_copy(step + 1, 1 - slot, r).start()
        sl = slice(c * fc, (c + 1) * fc)
        g = jnp.dot(x16[...], wg_ref[0, :, sl], preferred_element_type=F32)
        u = jnp.dot(x16[...], wu_ref[0, :, sl], preferred_element_type=F32)
        hid = (_silu(g) * u).astype(BF16)
        acc = acc + jnp.dot(hid, wd_ref[0, sl, :], preferred_element_type=F32)
        pl.semaphore_signal(fence, 1)
        pl.semaphore_wait(fence, 1)
    ye_ref[...] = acc.astype(BF16)

    @pl.when(step == nsteps - 1)
    def _():
        idx_copy(step + 2).wait()
        rows_wait(1 - slot)


def _experts(idx, h2, wg16, wu16, wd16, rm):
    e, _, cap = idx.shape
    nj = cap // rm
    t, d = h2.shape
    idx_steps = idx.reshape(e * nj, 1, rm)
    return pl.pallas_call(
        functools.partial(_expert_kernel, rm=rm, nj=nj, nsteps=e * nj, fc=512),
        grid=(e, nj),
        in_specs=[
            pl.BlockSpec(memory_space=pl.ANY),
            pl.BlockSpec(memory_space=pl.ANY),
            pl.BlockSpec((1, d, D_EXPERT), lambda i, j: (i, 0, 0)),
            pl.BlockSpec((1, d, D_EXPERT), lambda i, j: (i, 0, 0)),
            pl.BlockSpec((1, D_EXPERT, d), lambda i, j: (i, 0, 0)),
        ],
        out_specs=pl.BlockSpec((rm, d), lambda i, j: (i * nj + j, 0)),
        out_shape=jax.ShapeDtypeStruct((e * cap, d), BF16),
        scratch_shapes=[
            pltpu.SMEM((3, 1, rm), I32),
            pltpu.VMEM((2, rm, d), F32),
            pltpu.VMEM((rm, d), BF16),
            pltpu.SemaphoreType.DMA((3,)),
            pltpu.SemaphoreType.DMA((2,)),
            pltpu.SemaphoreType.REGULAR,
        ],
        compiler_params=_cparams(("arbitrary", "arbitrary")),
        name="expert",
    )(idx_steps, h2, wg16, wu16, wd16)


def _combine_kernel(starts_ref, q_ref, aff_ref, x1_ref, g2_ref, fnw_ref, ye_hbm, y_ref, ybuf, sem,
                    *, cap, win, short, tm):
    i = pl.program_id(0)
    n = pl.num_programs(0)
    slot = i % 2
    total = N_EXPERTS * cap

    def plan(tile):
        starts, all_short = [], None
        for e in range(N_EXPERTS):
            s0 = starts_ref[tile * N_EXPERTS + e]
            s1 = starts_ref[(tile + 1) * N_EXPERTS + e]
            gs = e * cap + s0
            a = pl.multiple_of(jnp.minimum((gs // BF16_ROWS) * BF16_ROWS, total - win), BF16_ROWS)
            fits = gs + (s1 - s0) - a <= short
            all_short = fits if all_short is None else jnp.logical_and(all_short, fits)
            starts.append(a)
        return starts, all_short

    def for_copies(tile, sl, act):
        starts, all_short = plan(tile)

        def run(rows):
            for e in range(N_EXPERTS):
                act(pltpu.make_async_copy(ye_hbm.at[pl.ds(starts[e], rows)], ybuf.at[sl, pl.ds(e * win, rows)],
                                          sem.at[sl]))

        pl.when(all_short)(lambda: run(short))
        pl.when(jnp.logical_not(all_short))(lambda: run(win))

    @pl.when(i == 0)
    def _():
        ybuf[...] = jnp.zeros_like(ybuf)
        for_copies(0, 0, lambda cp: cp.start())

    @pl.when(i + 1 < n)
    def _():
        for_copies(i + 1, 1 - slot, lambda cp: cp.start())

    for_copies(i, slot, lambda cp: cp.wait())

    starts, _ = plan(i)
    q = q_ref[...]
    aff = aff_ref[...]
    expert = lax.broadcasted_iota(I32, (N_EXPERTS, tm), 0)
    shift = jnp.zeros((N_EXPERTS, tm), I32)
    for e in range(N_EXPERTS):
        shift = jnp.where(expert == e, e * cap - starts[e], shift)
    row = jnp.where(q >= 0, q + shift, jnp.int32(-1))
    win_row = lax.broadcasted_iota(I32, (win, tm), 0)
    val_t = jnp.concatenate(
        [jnp.where(row[e:e + 1, :] == win_row, aff[e:e + 1, :], 0.0)
         for e in range(N_EXPERTS)], axis=0)
    val = val_t.T
    hi = val.astype(BF16)
    lo = (val - hi.astype(F32)).astype(BF16)
    both = jnp.dot(jnp.concatenate([hi, lo], axis=0), ybuf[slot], preferred_element_type=F32)
    out = x1_ref[...] + g2_ref[0] * (both[:tm] + both[tm:])
    y_ref[...] = _rms(out) * fnw_ref[...]


def _combine(starts_flat, q_slot, aff_t, x1, g2, final_norm_w, ye, cap, tm, tiles_per_batch):
    t, d = x1.shape
    win = tm + BF16_ROWS
    short = 4 * BF16_ROWS
    grid_spec = pltpu.PrefetchScalarGridSpec(
        num_scalar_prefetch=1,
        grid=(t // tm,),
        in_specs=[
            pl.BlockSpec((N_EXPERTS, tm), lambda i, s: (0, i)),
            pl.BlockSpec((N_EXPERTS, tm), lambda i, s: (0, i)),
            pl.BlockSpec((tm, d), lambda i, s: (i, 0)),
            pl.BlockSpec((1, 1, d), lambda i, s: (i // tiles_per_batch, 0, 0)),
            pl.BlockSpec((1, d), lambda i, s: (0, 0)),
            pl.BlockSpec(memory_space=pl.ANY),
        ],
        out_specs=pl.BlockSpec((tm, d), lambda i, s: (i, 0)),
        scratch_shapes=[
            pltpu.VMEM((2, N_EXPERTS * win, d), BF16),
            pltpu.SemaphoreType.DMA((2,)),
        ],
    )
    return pl.pallas_call(
        functools.partial(_combine_kernel, cap=cap, win=win, short=short, tm=tm),
        grid_spec=grid_spec,
        out_shape=jax.ShapeDtypeStruct((t, d), F32),
        compiler_params=_cparams(("arbitrary",)),
        name="combine",
    )(starts_flat, q_slot, aff_t, x1, g2, final_norm_w.reshape(1, d), ye)


def _rotate_half_cols(w):
    half = ROPE // 2
    return jnp.concatenate([-w[..., half:], w[..., :half]], axis=-1)


def _prep_weights(w_in, w_uq, w_ukv, w_out, w_gate, w_up, w_down):
    d = w_in.shape[0]
    offs = [Q_RANK, Q_RANK + KV_RANK, Q_RANK + KV_RANK + ROPE, Q_RANK + KV_RANK + ROPE + D_SSD,
            Q_RANK + KV_RANK + ROPE + D_SSD + CONV_DIM]
    w_cq, w_ckv, w_kr, w_z, w_xbc, w_dt = jnp.split(w_in, offs, axis=1)
    place = lambda w: jnp.pad(w, ((0, 0), (NOPE, HEAD_PAD - NOPE - ROPE)))
    win_ext = jnp.concatenate(
        [w_cq, w_ckv, place(w_kr), place(_rotate_half_cols(w_kr)), w_z, w_xbc,
         jnp.pad(w_dt, ((0, 0), (0, LANE - w_dt.shape[1])))], axis=1).astype(BF16)
    assert win_ext.shape == (d, _N_IN)

    wq = w_uq.reshape(Q_RANK, N_HEADS, NOPE + ROPE)
    pad_h = lambda w: jnp.pad(w, ((0, 0), (0, 0), (0, HEAD_PAD - NOPE - ROPE))).reshape(Q_RANK, N_HEADS * HEAD_PAD)
    wq_rot = jnp.concatenate([jnp.zeros_like(wq[..., :NOPE]), _rotate_half_cols(wq[..., NOPE:])], axis=-1)
    wq_ext = jnp.concatenate([pad_h(wq), pad_h(wq_rot)], axis=1).astype(BF16)

    wkv = w_ukv.reshape(KV_RANK, N_HEADS, NOPE + V_DIM)
    wk = jnp.pad(wkv[..., :NOPE], ((0, 0), (0, 0), (0, HEAD_PAD - NOPE))).reshape(KV_RANK, N_HEADS * HEAD_PAD)
    wv = jnp.pad(wkv[..., NOPE:], ((0, 0), (0, 0), (0, HEAD_PAD - V_DIM))).reshape(KV_RANK, N_HEADS * HEAD_PAD)
    wkv_ext = jnp.concatenate([wk, wv], axis=1).astype(BF16)
    return win_ext, wq_ext, wkv_ext, w_out.astype(BF16), w_gate.astype(BF16), w_up.astype(BF16), w_down.astype(BF16)


def _rope_table(seq):
    pos = jnp.arange(seq, dtype=F32)
    inv = 1.0 / (ROPE_THETA ** (jnp.arange(0, ROPE, 2, dtype=F32) / ROPE))
    ang = pos[:, None] * inv[None, :]
    cos = jnp.concatenate([jnp.cos(ang)] * 2, axis=-1)
    sin = jnp.concatenate([jnp.sin(ang)] * 2, axis=-1)
    zpad = jnp.zeros((seq, HEAD_PAD - NOPE - ROPE), F32)
    qs = (NOPE + ROPE) ** -0.5 * math.log2(math.e)
    cosq = jnp.concatenate([jnp.ones((seq, NOPE), F32), cos, zpad], axis=-1) * qs
    sinq = jnp.concatenate([jnp.zeros((seq, NOPE), F32), sin, zpad], axis=-1) * qs
    cosk = jnp.concatenate([jnp.zeros((seq, NOPE), F32), cos, zpad], axis=-1)
    sink = jnp.concatenate([jnp.zeros((seq, NOPE), F32), sin, zpad], axis=-1)
    return jnp.concatenate([cosq, sinq, cosk, sink], axis=-1)


def _conv_halo(xbc, tile):
    b, s, c = xbc.shape
    nc = s // tile
    r = xbc.reshape(b, nc, tile, c)
    z2 = jnp.zeros((b, 1, 2, c), xbc.dtype)
    prev = jnp.concatenate([z2, r[:, :-1, tile - 2:, :]], axis=1)
    nxt = jnp.concatenate([r[:, 1:, :2, :], z2], axis=1)
    return jnp.concatenate([prev, nxt, jnp.zeros((b, nc, 4, c), xbc.dtype)], axis=2)


def _trunk(x, mod, p):
    b, s, d = x.shape
    t = b * s
    cap = max(1, CAP_FACTOR * t // N_EXPERTS)
    sh1, sc1, g1, sh2, sc2, g2 = [m.reshape(b, 1, d) for m in jnp.split(mod, 6, axis=-1)]

    tm = min(512, s)
    q, k, v, z, xbc, dt = _inproj(x, sc1, sh1, p["norm1_w"], p["win_ext"], p["q_norm_w"], p["wq_ext"],
                                  p["kv_norm_w"], p["wkv_ext"], _rope_table(s), tm)
    ya = _attention(q, k, v, min(s, max(LANE, _SCORE_ELEMS // s)))
    yf, yb = _ssd(xbc, dt, p["conv_w8"], p["conv_b"], p["alog_flat"], p["dtb_flat"], p["dsk_flat"],
                  2 if b % 2 == 0 else 1)
    x1, h2, aff_t = _outproj(ya, yf, yb, z, x, g1, sc2, sh2, p["attn_norm_w"], p["ssd_norm_w"], p["wout16"],
                             p["norm2_w"], p["w_router"], tm)
    q_slot, idx, starts = _route(aff_t, cap)
    ye = _experts(idx, h2.reshape(t, d), p["wg16"], p["wu16"], p["wd16"], min(1024, cap))
    starts_flat = jnp.concatenate([starts.reshape(N_EXPERTS, t // LANE).T,
                                   jnp.full((1, N_EXPERTS), cap, I32)], axis=0).reshape(-1)
    y = _combine(starts_flat, q_slot.reshape(N_EXPERTS, t), aff_t, x1.reshape(t, d), g2, p["final_norm_w"], ye,
                 cap, LANE, s // LANE)
    return y.reshape(b, s, d)


def kernel(x_prompt, x_sample, c_prompt, c_sample, w_ada, b_ada, norm1_w, w_in, q_norm_w, w_uq, kv_norm_w, w_ukv,
           attn_norm_w, conv_w, conv_b, a_log, dt_bias, d_skip, ssd_norm_w, w_out, norm2_w, w_router, w_gate, w_up,
           w_down, final_norm_w):
    win_ext, wq_ext, wkv_ext, wout16, wg16, wu16, wd16 = _prep_weights(
        w_in[0], w_uq[0], w_ukv[0], w_out[0], w_gate[0], w_up[0], w_down[0])
    flat16 = lambda a: jnp.pad(a.reshape(1, 2 * SSD_HEADS), ((0, 0), (0, LANE - 2 * SSD_HEADS)))
    p = dict(
        norm1_w=norm1_w[0], win_ext=win_ext, q_norm_w=q_norm_w[0], wq_ext=wq_ext, kv_norm_w=kv_norm_w[0],
        wkv_ext=wkv_ext, attn_norm_w=attn_norm_w[0], ssd_norm_w=ssd_norm_w[0], wout16=wout16, norm2_w=norm2_w[0],
        w_router=w_router[0], wg16=wg16, wu16=wu16, wd16=wd16, final_norm_w=final_norm_w,
        conv_w8=jnp.pad(conv_w[0], ((0, 8 - D_CONV), (0, 0))), conv_b=conv_b[0].reshape(1, CONV_DIM),
        alog_flat=flat16(a_log[0]), dtb_flat=flat16(dt_bias[0]),
        dsk_flat=jnp.repeat(d_skip[0], SSD_P).reshape(1, D_SSD),
    )
    nbp = c_prompt.shape[0]
    mod = _mod(jnp.concatenate([c_prompt, c_sample], axis=0), w_ada[0], b_ada[0])
    return _trunk(x_prompt, mod[:nbp], p), _trunk(x_sample, mod[nbp:], p)
```
